```python
import jax, jax.numpy as jnp
from jax import lax
import numpy as np

D_MODEL = 1024
BATCH = 16
SEQ = 2048
DEPTH = 2
DEC_BATCH = 32
DEC_SEQ = 1
PAST_LEN = 16384
PAGE_SIZE = 128

N_HEADS = 16
HEAD_DIM = D_MODEL // N_HEADS
D_FF = ((8 * D_MODEL // 3 + 127) // 128) * 128
CONV_W = 3
N_A = DEPTH // 2
N_B = DEPTH - N_A
Q_BLOCK = 128
EPS = 1e-6
POOL_NUM = 5
POOL_DEN = 4
FORGET_BIAS_LO = 1.0
FORGET_BIAS_HI = 6.0
CACHE_FORGET_BIAS = 4.0

kernel_name = 'yoco_shortconv_fox_convglu_decoder_step'


def rmsnorm(x, g):
    xf = x.astype(jnp.float32)
    y = xf * lax.rsqrt(jnp.mean(xf * xf, axis=-1, keepdims=True) + EPS)
    return (y * g.astype(jnp.float32)).astype(x.dtype)


def modulate(x, g, shift, scale):
    return rmsnorm(x, g) * (1 + scale[:, None, :]) + shift[:, None, :]


def ada_params(c, w_ada, b_ada):
    mod = jax.nn.silu(c) @ w_ada + b_ada
    return jnp.split(mod, 6, axis=-1)


def causal_dwconv(u, w, past):
    full = jnp.concatenate([past.astype(u.dtype), u], axis=1)
    t = u.shape[1]
    y = full[:, 0:t] * w[0]
    for k in range(1, CONV_W):
        y = y + full[:, k:k + t] * w[k]
    return y, full[:, t:]


def shortconv_mixer(h, w_in, conv_w, w_out, past):
    b_gate, c_gate, xv = jnp.split(h @ w_in, 3, axis=-1)
    uc, new_state = causal_dwconv(c_gate * xv, conv_w, past)
    return (b_gate * uc) @ w_out, new_state


def conv_glu_ffn(h, w_up, conv_w, w_down, past):
    a, v = jnp.split(h @ w_up, 2, axis=-1)
    ac, new_state = causal_dwconv(a, conv_w, past)
    return (jax.nn.gelu(ac) * v) @ w_down, new_state


def shared_kv(x, kv_norm, w_kvf, b_f, k_norm):
    b, t, _ = x.shape
    kvf = rmsnorm(x, kv_norm) @ w_kvf
    k = rmsnorm(kvf[..., :D_MODEL].reshape(b, t, N_HEADS, HEAD_DIM), k_norm)
    v = kvf[..., D_MODEL:2 * D_MODEL].reshape(b, t, N_HEADS, HEAD_DIM)
    logf = jax.nn.log_sigmoid((kvf[..., 2 * D_MODEL:] + b_f).astype(jnp.float32))
    return k, v, logf


def fox_prompt(q, k, v, logf):
    t = q.shape[1]
    scale = HEAD_DIM ** -0.5
    fh = jnp.cumsum(logf, axis=1).transpose(0, 2, 1)
    outs = []
    for i in range(t // Q_BLOCK):
        q0, q1 = i * Q_BLOCK, (i + 1) * Q_BLOCK
        s = jnp.einsum('bqhd,bkhd->bhqk', q[:, q0:q1], k[:, :q1],
                       preferred_element_type=jnp.float32) * scale
        s = s + fh[:, :, q0:q1, None] - fh[:, :, None, :q1]
        causal = (q0 + jnp.arange(Q_BLOCK))[:, None] >= jnp.arange(q1)[None, :]
        p = jax.nn.softmax(jnp.where(causal, s, -jnp.inf), axis=-1)
        outs.append(jnp.einsum('bhqk,bkhd->bqhd', p.astype(v.dtype), v[:, :q1]))
    return jnp.concatenate(outs, axis=1)


def fox_sample(q, k_new, v_new, logf_new, cache_k, cache_v, cache_logf, page_table):
    dec_b, s_len = q.shape[0], q.shape[1]
    n_pages = page_table.shape[1]
    scale = HEAD_DIM ** -0.5
    logf_past = cache_logf[page_table].astype(jnp.float32).reshape(dec_b, n_pages * PAGE_SIZE, N_HEADS)
    suffix = jnp.flip(jnp.cumsum(jnp.flip(logf_past, axis=1), axis=1), axis=1) - logf_past
    suffix_pages = suffix.reshape(dec_b, n_pages, PAGE_SIZE, N_HEADS).transpose(1, 0, 3, 2)
    fn = jnp.cumsum(logf_new.astype(jnp.float32), axis=1).transpose(0, 2, 1)

    def online_update(carry, s, v_blk):
        m, l, acc = carry
        m_new = jnp.maximum(m, jnp.max(s, axis=-1))
        alpha = jnp.exp(m - m_new)
        p = jnp.exp(s - m_new[..., None])
        l = l * alpha + jnp.sum(p, axis=-1)
        acc = acc * alpha[..., None] + jnp.einsum('bhqk,bkhd->bhqd', p, v_blk.astype(jnp.float32))
        return (m_new, l, acc)

    def page_step(carry, xs):
        phys, suf = xs
        k_blk = cache_k[phys]
        v_blk = cache_v[phys]
        s = jnp.einsum('bqhd,bkhd->bhqk', q, k_blk, preferred_element_type=jnp.float32) * scale
        s = s + fn[..., :, None] + suf[:, :, None, :]
        return online_update(carry, s, v_blk), None

    init = (jnp.full((dec_b, N_HEADS, s_len), -jnp.inf, jnp.float32),
            jnp.zeros((dec_b, N_HEADS, s_len), jnp.float32),
            jnp.zeros((dec_b, N_HEADS, s_len, HEAD_DIM), jnp.float32))
    carry, _ = lax.scan(page_step, init, (page_table.T, suffix_pages))
    s = jnp.einsum('bqhd,bkhd->bhqk', q, k_new, preferred_element_type=jnp.float32) * scale
    s = s + fn[..., :, None] - fn[..., None, :]
    causal = jnp.tril(jnp.ones((s_len, s_len), dtype=bool))
    m, l, acc = online_update(carry, jnp.where(causal, s, -jnp.inf), v_new)
    return (acc / l[..., None]).transpose(0, 2, 1, 3).astype(q.dtype)


def trunk(x, c, conv_a_past, ffn_past, attn_fn, norm_mix, norm_ffn, w_ada, b_ada,
          w_in_a, conv_w_a, w_out_a, kv_norm, w_kvf, b_f, k_norm, w_q, q_norm, w_o,
          w_up, conv_w_ffn, w_down):
    b, t, _ = x.shape
    conv_a_new, ffn_new = [], []
    kv = None
    for layer in range(DEPTH):
        sh_m, sc_m, g_m, sh_f, sc_f, g_f = ada_params(c, w_ada[layer], b_ada[layer])
        h = modulate(x, norm_mix[layer], sh_m, sc_m)
        if layer < N_A:
            y, st = shortconv_mixer(h, w_in_a[layer], conv_w_a[layer], w_out_a[layer], conv_a_past[layer])
            conv_a_new.append(st)
        else:
            j = layer - N_A
            q = rmsnorm((h @ w_q[j]).reshape(b, t, N_HEADS, HEAD_DIM), q_norm[j])
            o = attn_fn(q, kv[0], kv[1], kv[2])
            y = o.reshape(b, t, D_MODEL) @ w_o[j]
        x = x + g_m[:, None, :] * y
        h = modulate(x, norm_ffn[layer], sh_f, sc_f)
        y, st = conv_glu_ffn(h, w_up[layer], conv_w_ffn[layer], w_down[layer], ffn_past[layer])
        ffn_new.append(st)
        x = x + g_f[:, None, :] * y
        if layer == N_A - 1:
            kv = shared_kv(x, kv_norm, w_kvf, b_f, k_norm)
    return x, jnp.stack(conv_a_new), jnp.stack(ffn_new), kv


def setup_inputs(seed: int = 0) -> dict:
    key = jax.random.key(seed)
    ks = jax.random.split(key, 32)
    D = D_MODEL
    n_pages = PAST_LEN // PAGE_SIZE
    n_pool = (DEC_BATCH * n_pages * POOL_NUM) // POOL_DEN

    def nrm(k, shape, s=1.0):
        return s * jax.random.normal(k, shape, jnp.float32)

    return {
        'x_prompt': nrm(ks[0], (BATCH, SEQ, D)),
        'x_sample': nrm(ks[1], (DEC_BATCH, DEC_SEQ, D)),
        'state_conv_a': nrm(ks[2], (N_A, DEC_BATCH, CONV_W - 1, D)),
        'state_ffn': nrm(ks[3], (DEPTH, DEC_BATCH, CONV_W - 1, D_FF)),
        'cache_k': nrm(ks[4], (n_pool, PAGE_SIZE, N_HEADS, HEAD_DIM)),
        'cache_v': nrm(ks[5], (n_pool, PAGE_SIZE, N_HEADS, HEAD_DIM)),
        'cache_logf': jax.nn.log_sigmoid(CACHE_FORGET_BIAS + nrm(ks[6], (n_pool, PAGE_SIZE, N_HEADS))),
        'page_table': jax.random.permutation(ks[7], n_pool)[:DEC_BATCH * n_pages].reshape(DEC_BATCH, n_pages).astype(jnp.int32),
        'c_prompt': nrm(ks[8], (BATCH, D)),
        'c_sample': nrm(ks[9], (DEC_BATCH, D)),
        'norm_mix': 1.0 + nrm(ks[10], (DEPTH, D), 0.02),
        'norm_ffn': 1.0 + nrm(ks[11], (DEPTH, D), 0.02),
        'w_ada': nrm(ks[12], (DEPTH, D, 6 * D), D ** -0.5),
        'b_ada': nrm(ks[13], (DEPTH, 6 * D), 0.02),
        'w_in_a': nrm(ks[14], (N_A, D, 3 * D), D ** -0.5),
        'conv_w_a': nrm(ks[15], (N_A, CONV_W, D), CONV_W ** -0.5),
        'w_out_a': nrm(ks[16], (N_A, D, D), D ** -0.5),
        'kv_norm': 1.0 + nrm(ks[17], (D,), 0.02),
        'w_kvf': nrm(ks[18], (D, 2 * D + N_HEADS), D ** -0.5),
        'b_f': jax.random.uniform(ks[19], (N_HEADS,), jnp.float32, FORGET_BIAS_LO, FORGET_BIAS_HI),
        'k_norm': 1.0 + nrm(ks[20], (HEAD_DIM,), 0.02),
        'w_q': nrm(ks[21], (N_B, D, D), D ** -0.5),
        'q_norm': 1.0 + nrm(ks[22], (N_B, HEAD_DIM), 0.02),
        'w_o': nrm(ks[23], (N_B, D, D), D ** -0.5),
        'w_up': nrm(ks[24], (DEPTH, D, 2 * D_FF), D ** -0.5),
        'conv_w_ffn': nrm(ks[25], (DEPTH, CONV_W, D_FF), CONV_W ** -0.5),
        'w_down': nrm(ks[26], (DEPTH, D_FF, D), D_FF ** -0.5),
    }


def reference(x_prompt, x_sample, state_conv_a, state_ffn, cache_k, cache_v, cache_logf,
              page_table, c_prompt, c_sample, norm_mix, norm_ffn, w_ada, b_ada, w_in_a,
              conv_w_a, w_out_a, kv_norm, w_kvf, b_f, k_norm, w_q, q_norm, w_o, w_up,
              conv_w_ffn, w_down):
    b = x_prompt.shape[0]

    def attn_sample(q, k, v, logf):
        return fox_sample(q, k, v, logf, cache_k, cache_v, cache_logf, page_table)

    zero_conv_a = jnp.zeros((N_A, b, CONV_W - 1, D_MODEL), x_prompt.dtype)
    zero_ffn = jnp.zeros((DEPTH, b, CONV_W - 1, D_FF), x_prompt.dtype)
    y_prompt, conv_a_p, ffn_p, kv_p = trunk(
        x_prompt, c_prompt, zero_conv_a, zero_ffn, fox_prompt, norm_mix, norm_ffn, w_ada, b_ada,
        w_in_a, conv_w_a, w_out_a, kv_norm, w_kvf, b_f, k_norm, w_q, q_norm, w_o,
        w_up, conv_w_ffn, w_down)
    y_sample, conv_a_s, ffn_s, kv_s = trunk(
        x_sample, c_sample, state_conv_a, state_ffn, attn_sample, norm_mix, norm_ffn, w_ada, b_ada,
        w_in_a, conv_w_a, w_out_a, kv_norm, w_kvf, b_f, k_norm, w_q, q_norm, w_o,
        w_up, conv_w_ffn, w_down)
    k_p, v_p, logf_p = kv_p
    k_s, v_s, logf_s = kv_s
    return (y_prompt, y_sample, conv_a_p, conv_a_s, ffn_p, ffn_s, k_p, v_p, logf_p, k_s, v_s, logf_s)
```

```python
import functools

import jax
import jax.numpy as jnp
from jax import lax
from jax.experimental import pallas as pl
from jax.experimental.pallas import tpu as pltpu

f32 = jnp.float32
bf16 = jnp.bfloat16

D_MODEL = 1024
N_HEADS = 16
HEAD_DIM = D_MODEL // N_HEADS
D_FF = 2816
PAGE = 128
EPS = 1e-6
ATTN_SCALE = HEAD_DIM ** -0.5

TM = 256
AUG = 16
HA = HEAD_DIM + AUG
DEC_PAGES = 4
ADA_TN = 1024
VMEM_LIMIT = 56 * 1024 * 1024


def _cparams(sem, vmem=None):
    return pltpu.CompilerParams(dimension_semantics=sem, vmem_limit_bytes=vmem)


def _resident(shape):
    nd = len(shape)
    return pl.BlockSpec(shape, lambda *_: (0,) * nd, pipeline_mode=pl.Buffered(1))


def _rms_rows(x):
    return x * lax.rsqrt(jnp.mean(x * x, axis=-1, keepdims=True) + EPS)


def _modulate_rows(x, g, shift, scale):
    return (_rms_rows(x) * g) * (1.0 + scale) + shift


def _gelu_tanh(x):
    c = 0.7978845608028654
    return x * (0.5 * (1.0 + jnp.tanh(c * (x + 0.044715 * (x * x * x)))))


def _log_sigmoid(x):
    return jnp.minimum(x, 0.0) - jnp.log1p(jnp.exp(-jnp.abs(x)))


def _split3(x):
    hi = x.astype(bf16)
    r1 = x - hi.astype(f32)
    mid = r1.astype(bf16)
    lo = (r1 - mid.astype(f32)).astype(bf16)
    return hi, mid, lo


def _dot3(x, m_bf16):
    hi, mid, lo = _split3(x)
    d = functools.partial(jnp.dot, preferred_element_type=f32)
    return (d(hi, m_bf16) + d(mid, m_bf16)) + d(lo, m_bf16)


def _ada_body(c_ref, w_ref, b_ref, o_ref):
    c = c_ref[...]
    s = c * (1.0 / (1.0 + jnp.exp(-c)))
    o_ref[0] = jnp.dot(s.astype(bf16), w_ref[0].astype(bf16), preferred_element_type=f32) + b_ref[0]


def _ada(c_all, w_ada, b_ada):
    n_layers, _, n_out = w_ada.shape
    rows = c_all.shape[0]
    return pl.pallas_call(
        _ada_body,
        out_shape=jax.ShapeDtypeStruct((n_layers, rows, n_out), f32),
        grid=(n_layers, n_out // ADA_TN),
        in_specs=[
            pl.BlockSpec((rows, D_MODEL), lambda l, j: (0, 0)),
            pl.BlockSpec((1, D_MODEL, ADA_TN), lambda l, j: (l, 0, j)),
            pl.BlockSpec((1, 1, ADA_TN), lambda l, j: (l, 0, j)),
        ],
        out_specs=pl.BlockSpec((1, rows, ADA_TN), lambda l, j: (l, 0, j)),
        compiler_params=_cparams(("parallel", "parallel")),
        name="ada",
    )(c_all, w_ada, b_ada.reshape(n_layers, 1, n_out))


def _conv_tile(u, cw, buf, first_tile):
    tm = u.shape[0]

    @pl.when(first_tile)
    def _():
        buf[0:8, :] = jnp.zeros((8, buf.shape[1]), f32)

    buf[8:tm + 8, :] = u
    out = buf[6:tm + 6, :] * cw[0:1] + buf[7:tm + 7, :] * cw[1:2] + u * cw[2:3]
    buf[0:8, :] = buf[tm:tm + 8, :]
    return out


def _mix_prompt_body(x_ref, mod_ref, g_ref, win_ref, cw_ref, wout_ref, o_ref, st_ref, ubuf):
    d = D_MODEL
    x = x_ref[0]
    m = mod_ref[0]
    h = _modulate_rows(x, g_ref[...], m[0:1], m[1:2])
    z = jnp.dot(h.astype(bf16), win_ref[...], preferred_element_type=f32)
    u = z[:, d:2 * d] * z[:, 2 * d:]
    uc = _conv_tile(u, cw_ref[...], ubuf, pl.program_id(1) == 0)
    st_ref[0] = ubuf[6:8, :]
    y = jnp.dot((z[:, :d] * uc).astype(bf16), wout_ref[...], preferred_element_type=f32)
    o_ref[0] = x + m[2:3] * y


def _mix_prompt(x, mod, g, w_in, cw, w_out):
    b, t, d = x.shape
    return pl.pallas_call(
        _mix_prompt_body,
        out_shape=(jax.ShapeDtypeStruct((b, t, d), f32), jax.ShapeDtypeStruct((b, 2, d), f32)),
        grid=(b, t // TM),
        in_specs=[
            pl.BlockSpec((1, TM, d), lambda i, j: (i, j, 0)),
            pl.BlockSpec((1, 6, d), lambda i, j: (i, 0, 0)),
            pl.BlockSpec((1, d), lambda i, j: (0, 0)),
            _resident((d, 3 * d)),
            pl.BlockSpec((3, d), lambda i, j: (0, 0)),
            _resident((d, d)),
        ],
        out_specs=(pl.BlockSpec((1, TM, d), lambda i, j: (i, j, 0)),
                   pl.BlockSpec((1, 2, d), lambda i, j: (i, 0, 0))),
        scratch_shapes=[pltpu.VMEM((TM + 8, d), f32)],
        compiler_params=_cparams(("parallel", "arbitrary"), VMEM_LIMIT),
        name="mix_prompt",
    )(x, mod, g, w_in, cw, w_out)


def _mix_sample_body(x_ref, mod_ref, g_ref, p0_ref, p1_ref, win_ref, cw_ref, wout_ref, o_ref, u_ref):
    d = D_MODEL
    x = x_ref[...]
    h = _modulate_rows(x, g_ref[...], mod_ref[0], mod_ref[1])
    z = jnp.dot(h.astype(bf16), win_ref[...], preferred_element_type=f32)
    u = z[:, d:2 * d] * z[:, 2 * d:]
    cw = cw_ref[...]
    uc = p0_ref[...] * cw[0:1] + p1_ref[...] * cw[1:2] + u * cw[2:3]
    u_ref[...] = u
    y = jnp.dot((z[:, :d] * uc).astype(bf16), wout_ref[...], preferred_element_type=f32)
    o_ref[...] = x + mod_ref[2] * y


def _mix_sample(x, mod, g, past0, past1, w_in, cw, w_out):
    n, d = x.shape
    return pl.pallas_call(
        _mix_sample_body,
        out_shape=(jax.ShapeDtypeStruct((n, d), f32), jax.ShapeDtypeStruct((n, d), f32)),
        compiler_params=_cparams(None, VMEM_LIMIT),
        name="mix_sample",
    )(x, mod, g, past0, past1, w_in, cw, w_out)


def _ffn_core(x, m_shift, m_scale, m_gate, g, wup_ref, wdown_ref, conv):
    h = _modulate_rows(x, g, m_shift, m_scale)
    up = jnp.dot(h.astype(bf16), wup_ref[...], preferred_element_type=f32)
    a = up[:, :D_FF]
    ac = conv(a)
    gl = _gelu_tanh(ac) * up[:, D_FF:]
    y = jnp.dot(gl.astype(bf16), wdown_ref[...], preferred_element_type=f32)
    return x + m_gate * y


def _ffn_prompt_body(with_attn, *refs):
    if with_attn:
        (x_ref, ot_ref, wo_ref, mod_ref, g_ref, wup_ref, cw_ref, wdown_ref, o_ref, st_ref, abuf) = refs
    else:
        (x_ref, mod_ref, g_ref, wup_ref, cw_ref, wdown_ref, o_ref, st_ref, abuf) = refs
    x = x_ref[0]
    m = mod_ref[0]
    if with_attn:
        attn = lax.dot_general(ot_ref[0], wo_ref[...], (((0,), (0,)), ((), ())),
                               preferred_element_type=f32)
        x = x + m[2:3] * attn
    conv = lambda a: _conv_tile(a, cw_ref[...], abuf, pl.program_id(1) == 0)
    o_ref[0] = _ffn_core(x, m[3:4], m[4:5], m[5:6], g_ref[...], wup_ref, wdown_ref, conv)
    st_ref[0] = abuf[6:8, :]


def _ffn_prompt(x, mod, g, w_up, cw, w_down, o_t=None, w_o=None):
    b, t, d = x.shape
    with_attn = o_t is not None
    xspec = pl.BlockSpec((1, TM, d), lambda i, j: (i, j, 0))
    in_specs = [xspec]
    args = [x]
    if with_attn:
        in_specs += [pl.BlockSpec((1, d, TM), lambda i, j: (i, 0, j)), _resident((d, d))]
        args += [o_t, w_o]
    in_specs += [
        pl.BlockSpec((1, 6, d), lambda i, j: (i, 0, 0)),
        pl.BlockSpec((1, d), lambda i, j: (0, 0)),
        _resident((d, 2 * D_FF)),
        pl.BlockSpec((3, D_FF), lambda i, j: (0, 0)),
        _resident((D_FF, d)),
    ]
    args += [mod, g, w_up, cw, w_down]
    return pl.pallas_call(
        functools.partial(_ffn_prompt_body, with_attn),
        out_shape=(jax.ShapeDtypeStruct((b, t, d), f32), jax.ShapeDtypeStruct((b, 2, D_FF), f32)),
        grid=(b, t // TM),
        in_specs=in_specs,
        out_specs=(xspec, pl.BlockSpec((1, 2, D_FF), lambda i, j: (i, 0, 0))),
        scratch_shapes=[pltpu.VMEM((TM + 8, D_FF), f32)],
        compiler_params=_cparams(("parallel", "arbitrary"), VMEM_LIMIT),
        name="ffn_attn_prompt" if with_attn else "ffn_prompt",
    )(*args)


def _ffn_sample_body(with_attn, *refs):
    if with_attn:
        (x_ref, o_ref_in, wo_ref, mod_ref, g_ref, p0_ref, p1_ref, wup_ref, cw_ref, wdown_ref,
         y_ref, a_ref) = refs
    else:
        (x_ref, mod_ref, g_ref, p0_ref, p1_ref, wup_ref, cw_ref, wdown_ref, y_ref, a_ref) = refs
    x = x_ref[...]
    if with_attn:
        attn = jnp.dot(o_ref_in[...].astype(bf16), wo_ref[...], preferred_element_type=f32)
        x = x + mod_ref[2] * attn
    cw = cw_ref[...]

    def conv(a):
        a_ref[...] = a
        return p0_ref[...] * cw[0:1] + p1_ref[...] * cw[1:2] + a * cw[2:3]

    y_ref[...] = _ffn_core(x, mod_ref[3], mod_ref[4], mod_ref[5], g_ref[...], wup_ref, wdown_ref, conv)


def _ffn_sample(x, mod, g, past0, past1, w_up, cw, w_down, o=None, w_o=None):
    n, d = x.shape
    with_attn = o is not None
    args = [x] + ([o, w_o] if with_attn else []) + [mod, g, past0, past1, w_up, cw, w_down]
    return pl.pallas_call(
        functools.partial(_ffn_sample_body, with_attn),
        out_shape=(jax.ShapeDtypeStruct((n, d), f32), jax.ShapeDtypeStruct((n, D_FF), f32)),
        compiler_params=_cparams(None, VMEM_LIMIT),
        name="ffn_attn_sample" if with_attn else "ffn_sample",
    )(*args)


def _aug_rows(f_row, negate_first):
    hi, mid, lo = (p.astype(f32) for p in _split3(f_row))
    tm = f_row.shape[1]
    row = lax.broadcasted_iota(jnp.int32, (AUG, tm), 0)
    one = jnp.ones((AUG, tm), f32)
    zero = jnp.zeros((AUG, tm), f32)
    if negate_first:
        parts = (-hi, -mid, -lo, one, one, one)
    else:
        parts = (one, one, one, hi, mid, lo)
    out = zero
    for r, p in enumerate(parts):
        out = jnp.where(row == r, jnp.broadcast_to(p, (AUG, tm)), out)
    return out


def _kvq_prompt_body(x_ref, modc_ref, pcol_ref, bf_ref, tri_ref, wk_ref, wv_ref, wf_ref, wq_ref,
                     kt_ref, vt_ref, lf_ref, qa_ref, ka_ref, vb_ref, carry):
    tm = x_ref.shape[1]
    xt = x_ref[0].T
    xn = xt * lax.rsqrt(jnp.mean(xt * xt, axis=0, keepdims=True) + EPS)
    pc = pcol_ref[...]
    mc = modc_ref[0]
    hk = (xn * pc[:, 0:1]).astype(bf16)
    hq = ((xn * pc[:, 1:2]) * (1.0 + mc[:, 1:2]) + mc[:, 0:1]).astype(bf16)
    kt = jnp.dot(wk_ref[...], hk, preferred_element_type=f32)
    vt = jnp.dot(wv_ref[...], hk, preferred_element_type=f32)
    ft = jnp.dot(wf_ref[...], hk, preferred_element_type=f32)
    qt = jnp.dot(wq_ref[...], hq, preferred_element_type=f32)

    logf = _log_sigmoid(ft + bf_ref[...])
    lf_ref[0] = logf

    @pl.when(pl.program_id(1) == 0)
    def _():
        carry[...] = jnp.zeros(carry.shape, f32)

    cum = _dot3(logf, tri_ref[...]) + carry[...]
    carry[...] = cum[:, tm - 1:tm]

    vt_ref[0] = vt
    vb_ref[0, :, 0] = vt.astype(bf16).reshape(N_HEADS, HEAD_DIM, tm)
    for h in range(N_HEADS):
        r0, r1 = h * HEAD_DIM, (h + 1) * HEAD_DIM
        kh = kt[r0:r1]
        kh = kh * lax.rsqrt(jnp.mean(kh * kh, axis=0, keepdims=True) + EPS) * pc[r0:r1, 2:3]
        kt_ref[0, r0:r1, :] = kh
        qh = qt[r0:r1]
        qh = qh * lax.rsqrt(jnp.mean(qh * qh, axis=0, keepdims=True) + EPS) * pc[r0:r1, 3:4]
        f_row = cum[h:h + 1]
        ka_ref[0, h, 0, 0:HEAD_DIM, :] = kh.astype(bf16)
        ka_ref[0, h, 0, HEAD_DIM:HA, :] = _aug_rows(f_row, True).astype(bf16)
        qa_ref[0, h, 0, 0:HEAD_DIM, :] = (qh * ATTN_SCALE).astype(bf16)
        qa_ref[0, h, 0, HEAD_DIM:HA, :] = _aug_rows(f_row, False).astype(bf16)


def _kvq_prompt(x, modc, pcol, bf_col, tri, wk_t, wv_t, wf_t, wq_t):
    b, t, d = x.shape
    nt = t // TM
    out_shape = (
        jax.ShapeDtypeStruct((b, d, t), f32),
        jax.ShapeDtypeStruct((b, d, t), f32),
        jax.ShapeDtypeStruct((b, N_HEADS, t), f32),
        jax.ShapeDtypeStruct((b, N_HEADS, nt, HA, TM), bf16),
        jax.ShapeDtypeStruct((b, N_HEADS, nt, HA, TM), bf16),
        jax.ShapeDtypeStruct((b, N_HEADS, nt, HEAD_DIM, TM), bf16),
    )
    ct = pl.BlockSpec((1, d, TM), lambda i, j: (i, 0, j))
    return pl.pallas_call(
        _kvq_prompt_body,
        out_shape=out_shape,
        grid=(b, nt),
        in_specs=[
            pl.BlockSpec((1, TM, d), lambda i, j: (i, j, 0)),
            pl.BlockSpec((1, d, 2), lambda i, j: (i, 0, 0)),
            pl.BlockSpec((d, 4), lambda i, j: (0, 0)),
            pl.BlockSpec((N_HEADS, 1), lambda i, j: (0, 0)),
            pl.BlockSpec((TM, TM), lambda i, j: (0, 0)),
            _resident((d, d)), _resident((d, d)), _resident((N_HEADS, d)), _resident((d, d)),
        ],
        out_specs=(
            ct, ct,
            pl.BlockSpec((1, N_HEADS, TM), lambda i, j: (i, 0, j)),
            pl.BlockSpec((1, N_HEADS, 1, HA, TM), lambda i, j: (i, 0, j, 0, 0)),
            pl.BlockSpec((1, N_HEADS, 1, HA, TM), lambda i, j: (i, 0, j, 0, 0)),
            pl.BlockSpec((1, N_HEADS, 1, HEAD_DIM, TM), lambda i, j: (i, 0, j, 0, 0)),
        ),
        scratch_shapes=[pltpu.VMEM((N_HEADS, 1), f32)],
        compiler_params=_cparams(("parallel", "arbitrary"), VMEM_LIMIT),
        name="kvq_prompt",
    )(x, modc, pcol, bf_col, tri, wk_t, wv_t, wf_t, wq_t)


def _attn_prompt_body(qa_ref, ka_ref, vb_ref, o_ref):
    i = pl.program_id(2)
    qa = qa_ref[0, 0, 0]
    tq = qa.shape[1]

    def block(kb, carry, masked):
        m, l, acc = carry
        s = lax.dot_general(ka_ref[0, 0, kb], qa, (((0,), (0,)), ((), ())),
                            preferred_element_type=f32)
        if masked:
            key = lax.broadcasted_iota(jnp.int32, s.shape, 0)
            qry = lax.broadcasted_iota(jnp.int32, s.shape, 1)
            s = jnp.where(key <= qry, s, -jnp.inf)
        m_new = jnp.maximum(m, jnp.max(s, axis=0, keepdims=True))
        alpha = jnp.exp(m - m_new)
        p = jnp.exp(s - m_new)
        l = l * alpha + jnp.sum(p, axis=0, keepdims=True)
        acc = acc * alpha + jnp.dot(vb_ref[0, 0, kb], p.astype(bf16), preferred_element_type=f32)
        return m_new, l, acc

    init = (jnp.full((1, tq), -jnp.inf, f32), jnp.zeros((1, tq), f32), jnp.zeros((HEAD_DIM, tq), f32))
    carry = lax.fori_loop(0, i, lambda kb, c: block(kb, c, False), init)
    _, l, acc = block(i, carry, True)
    o_ref[0, 0] = (acc / l).astype(bf16)


def _attn_prompt(qa, ka, vb):
    b, nh, nt, _, _ = qa.shape
    return pl.pallas_call(
        _attn_prompt_body,
        out_shape=jax.ShapeDtypeStruct((b, nh, HEAD_DIM, nt * TM), bf16),
        grid=(b, nh, nt),
        in_specs=[
            pl.BlockSpec((1, 1, 1, HA, TM), lambda i, h, j: (i, h, j, 0, 0)),
            pl.BlockSpec((1, 1, nt, HA, TM), lambda i, h, j: (i, h, 0, 0, 0)),
            pl.BlockSpec((1, 1, nt, HEAD_DIM, TM), lambda i, h, j: (i, h, 0, 0, 0)),
        ],
        out_specs=pl.BlockSpec((1, 1, HEAD_DIM, TM), lambda i, h, j: (i, h, 0, j)),
        compiler_params=_cparams(("parallel", "parallel", "parallel")),
        name="attn_prompt",
    )(qa, ka, vb)


def _head_rms_rows(x, gmat_ref, w_row):
    ms = _dot3(x * x, gmat_ref[...])
    return x * lax.rsqrt(ms + EPS) * w_row


def _kvq_sample_body(x_ref, mod_ref, prow_ref, bf_ref, gmat_ref, wkvf_ref, wq_ref,
                     k_ref, v_ref, lf_ref, q_ref):
    d = D_MODEL
    xn = _rms_rows(x_ref[...])
    pr = prow_ref[...]
    hk = (xn * pr[0:1]).astype(bf16)
    hq = ((xn * pr[1:2]) * (1.0 + mod_ref[1]) + mod_ref[0]).astype(bf16)
    kvf = jnp.dot(hk, wkvf_ref[...], preferred_element_type=f32)
    qr = jnp.dot(hq, wq_ref[...], preferred_element_type=f32)
    k_ref[...] = _head_rms_rows(kvf[:, :d], gmat_ref, pr[2:3])
    v_ref[...] = kvf[:, d:2 * d]
    lf_ref[...] = _log_sigmoid(kvf[:, 2 * d:] + bf_ref[...])
    q_ref[...] = _head_rms_rows(qr, gmat_ref, pr[3:4])


def _kvq_sample(x, mod, prow, bf_row, gmat, w_kvf_pad, w_q):
    n, d = x.shape
    return pl.pallas_call(
        _kvq_sample_body,
        out_shape=(jax.ShapeDtypeStruct((n, d), f32), jax.ShapeDtypeStruct((n, d), f32),
                   jax.ShapeDtypeStruct((n, 128), f32), jax.ShapeDtypeStruct((n, d), f32)),
        compiler_params=_cparams(None, VMEM_LIMIT),
        name="kvq_sample",
    )(x, mod, prow, bf_row, gmat, w_kvf_pad, w_q)


def _to_cols(row):
    return jnp.broadcast_to(row, (128, D_MODEL)).T.reshape(N_HEADS, HEAD_DIM, 128)


def _attn_sample_body(pt_ref, q_ref, kn_ref, vn_ref, fn_ref, rev_ref, *refs):
    g = DEC_PAGES
    k_refs, v_refs, lf_refs = refs[0:g], refs[g:2 * g], refs[2 * g:3 * g]
    o_ref = refs[3 * g]
    qcol, m_ref, l_ref, suf_ref, acc_ref = refs[3 * g + 1:]
    step = pl.program_id(1)
    fn = fn_ref[0]

    @pl.when(step == 0)
    def _():
        qcol[...] = _to_cols(q_ref[0])
        m_ref[...] = jnp.full(m_ref.shape, -jnp.inf, f32)
        l_ref[...] = jnp.zeros(l_ref.shape, f32)
        suf_ref[...] = jnp.zeros(suf_ref.shape, f32)
        acc_ref[...] = jnp.zeros(acc_ref.shape, f32)

    q3 = qcol[...]
    carry = suf_ref[...]
    scores = []
    for j in range(g):
        lf = lf_refs[j][0]
        suf = _dot3(lf, rev_ref[...]) + carry
        carry = carry + jnp.sum(lf, axis=1, keepdims=True)
        s = jnp.sum(k_refs[j][0] * q3, axis=1) * ATTN_SCALE
        scores.append(s + fn + suf)
    suf_ref[...] = carry

    m_old = m_ref[...]
    m_new = m_old
    for s in scores:
        m_new = jnp.maximum(m_new, jnp.max(s, axis=1, keepdims=True))
    alpha = jnp.exp(m_old - m_new)
    l_new = l_ref[...] * alpha
    acc = acc_ref[...] * alpha[:, :, None]
    for j in range(g):
        p = jnp.exp(scores[j] - m_new)
        l_new = l_new + jnp.sum(p, axis=1, keepdims=True)
        acc = acc + v_refs[j][0] * p[:, None, :]
    m_ref[...] = m_new
    l_ref[...] = l_new
    acc_ref[...] = acc

    @pl.when(step == pl.num_programs(1) - 1)
    def _():
        kn3 = _to_cols(kn_ref[0])
        vn3 = _to_cols(vn_ref[0])
        s_new = jnp.sum(kn3 * q3, axis=1)[:, 0:1] * ATTN_SCALE
        s_new = (s_new + fn) - fn
        m_fin = jnp.maximum(m_new, s_new)
        a_fin = jnp.exp(m_new - m_fin)
        p_new = jnp.exp(s_new - m_fin)
        l_fin = l_new * a_fin + p_new
        o_un = jnp.sum(acc, axis=2, keepdims=True) * a_fin[:, :, None] + vn3[:, :, 0:1] * p_new[:, :, None]
        o_col = (o_un / l_fin[:, :, None]).reshape(D_MODEL, 1)
        o_ref[0] = jnp.broadcast_to(o_col, (D_MODEL, 128)).T[0:1, :]


def _attn_sample(page_table, q, k_new, v_new, logf_new, rev, cache_kt, cache_vt, cache_lft):
    n, n_pages = page_table.shape
    g = DEC_PAGES
    steps = n_pages // g
    row = pl.BlockSpec((1, 1, D_MODEL), lambda b, s, pt: (b, 0, 0))

    def page_spec(shape, j):
        nd = len(shape)
        return pl.BlockSpec(
            (1,) + shape,
            lambda b, s, pt: (pt[b, n_pages - 1 - (s * g + j)],) + (0,) * nd)

    in_specs = [row, row, row,
                pl.BlockSpec((1, N_HEADS, 1), lambda b, s, pt: (b, 0, 0)),
                pl.BlockSpec((PAGE, PAGE), lambda b, s, pt: (0, 0))]
    in_specs += [page_spec((N_HEADS, HEAD_DIM, PAGE), j) for j in range(g)]
    in_specs += [page_spec((N_HEADS, HEAD_DIM, PAGE), j) for j in range(g)]
    in_specs += [page_spec((N_HEADS, PAGE), j) for j in range(g)]
    grid_spec = pltpu.PrefetchScalarGridSpec(
        num_scalar_prefetch=1,
        grid=(n, steps),
        in_specs=in_specs,
        out_specs=row,
        scratch_shapes=[
            pltpu.VMEM((N_HEADS, HEAD_DIM, 128), f32),
            pltpu.VMEM((N_HEADS, 1), f32),
            pltpu.VMEM((N_HEADS, 1), f32),
            pltpu.VMEM((N_HEADS, 1), f32),
            pltpu.VMEM((N_HEADS, HEAD_DIM, PAGE), f32),
        ],
    )
    return pl.pallas_call(
        _attn_sample_body,
        out_shape=jax.ShapeDtypeStruct((n, 1, D_MODEL), f32),
        grid_spec=grid_spec,
        compiler_params=_cparams(("parallel", "arbitrary"), VMEM_LIMIT),
        name="attn_sample",
    )(page_table, q[:, None, :], k_new[:, None, :], v_new[:, None, :], logf_new[:, :, None], rev,
      *([cache_kt] * g), *([cache_vt] * g), *([cache_lft] * g))


def kernel(x_prompt, x_sample, state_conv_a, state_ffn, cache_k, cache_v, cache_logf, page_table,
           c_prompt, c_sample, norm_mix, norm_ffn, w_ada, b_ada, w_in_a, conv_w_a, w_out_a, kv_norm,
           w_kvf, b_f, k_norm, w_q, q_norm, w_o, w_up, conv_w_ffn, w_down):
    d = D_MODEL
    nb, t, _ = x_prompt.shape
    ns = x_sample.shape[0]

    w_in = w_in_a[0].astype(bf16)
    w_out = w_out_a[0].astype(bf16)
    w_up_b = w_up.astype(bf16)
    w_down_b = w_down.astype(bf16)
    w_o_b = w_o[0].astype(bf16)
    w_q_b = w_q[0].astype(bf16)
    w_kvf_b = w_kvf.astype(bf16)
    wk_t = w_kvf_b[:, :d].T
    wv_t = w_kvf_b[:, d:2 * d].T
    wf_t = w_kvf_b[:, 2 * d:].T
    wq_t = w_q_b.T
    w_kvf_pad = jnp.pad(w_kvf_b, ((0, 0), (0, 128 - N_HEADS)))
    k_norm_d = jnp.tile(k_norm, N_HEADS)
    q_norm_d = jnp.tile(q_norm[0], N_HEADS)
    prow = jnp.stack([kv_norm, norm_mix[1], k_norm_d, q_norm_d])
    pcol = prow.T
    bf_row = jnp.pad(b_f, (0, 128 - N_HEADS))[None, :]
    bf_col = b_f[:, None]
    idx = jnp.arange(TM)
    tri = (idx[:, None] <= idx[None, :]).astype(bf16)
    pidx = jnp.arange(PAGE)
    rev = (pidx[:, None] > pidx[None, :]).astype(bf16)
    lane_head = jnp.arange(d) // HEAD_DIM
    gmat = ((lane_head[:, None] == lane_head[None, :]).astype(f32) / HEAD_DIM).astype(bf16)

    mod = _ada(jnp.concatenate([c_prompt, c_sample], axis=0), w_ada, b_ada)
    mod_p = mod[:, :nb].reshape(2, nb, 6, d)
    mod_s = mod[:, nb:].reshape(2, ns, 6, d).transpose(0, 2, 1, 3)
    modc_p = mod_p[1, :, 0:2].transpose(0, 2, 1)

    x1, conv_a_p = _mix_prompt(x_prompt, mod_p[0], norm_mix[0:1], w_in, conv_w_a[0], w_out)
    x2, ffn0_p = _ffn_prompt(x1, mod_p[0], norm_ffn[0:1], w_up_b[0], conv_w_ffn[0], w_down_b[0])
    kt_p, vt_p, lft_p, qa, ka, vb = _kvq_prompt(x2, modc_p, pcol, bf_col, tri, wk_t, wv_t, wf_t, wq_t)
    o_t = _attn_prompt(qa, ka, vb).reshape(nb, d, t)
    y_p, ffn1_p = _ffn_prompt(x2, mod_p[1], norm_ffn[1:2], w_up_b[1], conv_w_ffn[1], w_down_b[1],
                              o_t=o_t, w_o=w_o_b)

    xs = x_sample[:, 0, :]
    xs1, u_s = _mix_sample(xs, mod_s[0], norm_mix[0:1], state_conv_a[0, :, 0], state_conv_a[0, :, 1],
                           w_in, conv_w_a[0], w_out)
    xs2, a0_s = _ffn_sample(xs1, mod_s[0], norm_ffn[0:1], state_ffn[0, :, 0], state_ffn[0, :, 1],
                            w_up_b[0], conv_w_ffn[0], w_down_b[0])
    k_s, v_s, lf_s, q_s = _kvq_sample(xs2, mod_s[1], prow, bf_row, gmat, w_kvf_pad, w_q_b)
    lf_s = lf_s[:, :N_HEADS]
    cache_kt = cache_k.transpose(0, 2, 3, 1)
    cache_vt = cache_v.transpose(0, 2, 3, 1)
    cache_lft = cache_logf.transpose(0, 2, 1)
    o_s = _attn_sample(page_table, q_s, k_s, v_s, lf_s, rev, cache_kt, cache_vt, cache_lft)[:, 0, :]
    y_s, a1_s = _ffn_sample(xs2, mod_s[1], norm_ffn[1:2], state_ffn[1, :, 0], state_ffn[1, :, 1],
                            w_up_b[1], conv_w_ffn[1], w_down_b[1], o=o_s, w_o=w_o_b)

    conv_a_s = jnp.stack([state_conv_a[0, :, 1], u_s], axis=1)[None]
    ffn_p = jnp.stack([ffn0_p, ffn1_p])
    ffn_s = jnp.stack([jnp.stack([state_ffn[0, :, 1], a0_s], axis=1),
                       jnp.stack([state_ffn[1, :, 1], a1_s], axis=1)])
    k_p = kt_p.reshape(nb, N_HEADS, HEAD_DIM, t).transpose(0, 3, 1, 2)
    v_p = vt_p.reshape(nb, N_HEADS, HEAD_DIM, t).transpose(0, 3, 1, 2)
    logf_p = lft_p.transpose(0, 2, 1)
    return (y_p, y_s[:, None, :], conv_a_p[None], conv_a_s, ffn_p, ffn_s,
            k_p, v_p, logf_p,
            k_s.reshape(ns, 1, N_HEADS, HEAD_DIM), v_s.reshape(ns, 1, N_HEADS, HEAD_DIM),
            lf_s[:, None, :])
```

```python
import functools

import jax
import jax.numpy as jnp
from jax import lax
from jax.experimental import pallas as pl
from jax.experimental.pallas import tpu as pltpu

f32 = jnp.float32
bf16 = jnp.bfloat16

D_MODEL = 1024
N_HEADS = 16
HEAD_DIM = D_MODEL // N_HEADS
D_FF = 2816
PAGE = 128
EPS = 1e-6
ATTN_SCALE = HEAD_DIM ** -0.5
LOG2E = 1.4426950408889634

TM = 256
TMX = 512
TMF = 256
CHUNK = 256
AUG = 16
HA = HEAD_DIM + AUG
ATT_HG = 2
DEC_PAGES = 8
ADA_TN = 1024
VMEM_LIMIT = 56 * 1024 * 1024


def _cparams(sem, vmem=None):
    return pltpu.CompilerParams(dimension_semantics=sem, vmem_limit_bytes=vmem)


def _resident(shape):
    nd = len(shape)
    return pl.BlockSpec(shape, lambda *_: (0,) * nd, pipeline_mode=pl.Buffered(1))


def _rms_rows(x):
    return x * lax.rsqrt(jnp.mean(x * x, axis=-1, keepdims=True) + EPS)


def _modulate_rows(x, g, shift, scale):
    return (_rms_rows(x) * g) * (1.0 + scale) + shift


def _gelu_tanh(x):
    c = 0.7978845608028654
    return x * (0.5 * (1.0 + jnp.tanh(c * (x + 0.044715 * (x * x * x)))))


def _log_sigmoid(x):
    return jnp.minimum(x, 0.0) - jnp.log1p(jnp.exp(-jnp.abs(x)))


def _split3(x):
    hi = x.astype(bf16)
    r1 = x - hi.astype(f32)
    mid = r1.astype(bf16)
    lo = (r1 - mid.astype(f32)).astype(bf16)
    return hi, mid, lo


def _dot3(x, m_bf16):
    hi, mid, lo = _split3(x)
    d = functools.partial(jnp.dot, preferred_element_type=f32)
    return (d(hi, m_bf16) + d(mid, m_bf16)) + d(lo, m_bf16)


def _ada_body(c_ref, w_ref, b_ref, o_ref):
    c = c_ref[...]
    s = c * (1.0 / (1.0 + jnp.exp(-c)))
    o_ref[0] = jnp.dot(s.astype(bf16), w_ref[0].astype(bf16), preferred_element_type=f32) + b_ref[0]


def _ada(c_all, w_ada, b_ada):
    n_layers, _, n_out = w_ada.shape
    rows = c_all.shape[0]
    return pl.pallas_call(
        _ada_body,
        out_shape=jax.ShapeDtypeStruct((n_layers, rows, n_out), f32),
        grid=(n_layers, n_out // ADA_TN),
        in_specs=[
            pl.BlockSpec((rows, D_MODEL), lambda l, j: (0, 0)),
            pl.BlockSpec((1, D_MODEL, ADA_TN), lambda l, j: (l, 0, j)),
            pl.BlockSpec((1, 1, ADA_TN), lambda l, j: (l, 0, j)),
        ],
        out_specs=pl.BlockSpec((1, rows, ADA_TN), lambda l, j: (l, 0, j)),
        compiler_params=_cparams(("parallel", "parallel")),
        name="ada",
    )(c_all, w_ada, b_ada.reshape(n_layers, 1, n_out))


def _conv_start(buf, first_tile):
    @pl.when(first_tile)
    def _():
        buf[0:8, :] = jnp.zeros((8, buf.shape[1]), f32)


def _conv_chunk(u, cw, buf, c0):
    tm, w = u.shape
    cols = slice(c0, c0 + w)
    buf[8:tm + 8, cols] = u
    return (buf[6:tm + 6, cols] * cw[0:1, cols] + buf[7:tm + 7, cols] * cw[1:2, cols]
            + u * cw[2:3, cols])


def _conv_finish(buf, st_ref, tm):
    st_ref[0] = buf[tm + 6:tm + 8, :]
    buf[0:8, :] = buf[tm:tm + 8, :]


def _chunk_pipeline(n, produce, consume):
    pending = [produce(0)]
    total = None
    for c in range(n):
        if c + 1 < n:
            pending.append(produce(c + 1))
        part = consume(c, *pending.pop(0))
        total = part if total is None else total + part
    return total


def _mix_prompt_body(x_ref, mod_ref, g_ref, win_ref, cw_ref, wout_ref, o_ref, st_ref, ubuf):
    d = D_MODEL
    tm = x_ref.shape[1]
    x = x_ref[0]
    m = mod_ref[0]
    h = _modulate_rows(x, g_ref[...], m[0:1], m[1:2]).astype(bf16)
    z = jnp.dot(h, win_ref[...], preferred_element_type=f32)
    _conv_start(ubuf, pl.program_id(1) == 0)
    uc = _conv_chunk(z[:, d:2 * d] * z[:, 2 * d:], cw_ref[...], ubuf, 0)
    _conv_finish(ubuf, st_ref, tm)
    y = jnp.dot((z[:, :d] * uc).astype(bf16), wout_ref[...], preferred_element_type=f32)
    o_ref[0] = x + m[2:3] * y


def _mix_prompt(x, mod, g, w_in, cw, w_out):
    b, t, d = x.shape
    return pl.pallas_call(
        _mix_prompt_body,
        out_shape=(jax.ShapeDtypeStruct((b, t, d), f32), jax.ShapeDtypeStruct((b, 2, d), f32)),
        grid=(b, t // TMX),
        in_specs=[
            pl.BlockSpec((1, TMX, d), lambda i, j: (i, j, 0)),
            pl.BlockSpec((1, 6, d), lambda i, j: (i, 0, 0)),
            pl.BlockSpec((1, d), lambda i, j: (0, 0)),
            _resident((d, 3 * d)),
            pl.BlockSpec((3, d), lambda i, j: (0, 0)),
            _resident((d, d)),
        ],
        out_specs=(pl.BlockSpec((1, TMX, d), lambda i, j: (i, j, 0)),
                   pl.BlockSpec((1, 2, d), lambda i, j: (i, 0, 0))),
        scratch_shapes=[pltpu.VMEM((TMX + 8, d), f32)],
        compiler_params=_cparams(("parallel", "arbitrary"), VMEM_LIMIT),
        name="mix_prompt",
    )(x, mod, g, w_in, cw, w_out)


def _mix_sample_body(x_ref, mod_ref, g_ref, p0_ref, p1_ref, win_ref, cw_ref, wout_ref, o_ref, u_ref):
    d = D_MODEL
    x = x_ref[...]
    h = _modulate_rows(x, g_ref[...], mod_ref[0], mod_ref[1])
    z = jnp.dot(h.astype(bf16), win_ref[...], preferred_element_type=f32)
    u = z[:, d:2 * d] * z[:, 2 * d:]
    cw = cw_ref[...]
    uc = p0_ref[...] * cw[0:1] + p1_ref[...] * cw[1:2] + u * cw[2:3]
    u_ref[...] = u
    y = jnp.dot((z[:, :d] * uc).astype(bf16), wout_ref[...], preferred_element_type=f32)
    o_ref[...] = x + mod_ref[2] * y


def _mix_sample(x, mod, g, past0, past1, w_in, cw, w_out):
    n, d = x.shape
    return pl.pallas_call(
        _mix_sample_body,
        out_shape=(jax.ShapeDtypeStruct((n, d), f32), jax.ShapeDtypeStruct((n, d), f32)),
        compiler_params=_cparams(None, VMEM_LIMIT),
        name="mix_sample",
    )(x, mod, g, past0, past1, w_in, cw, w_out)


def _ffn_core(x, m_shift, m_scale, m_gate, g, wup_ref, wdown_ref, conv):
    h = _modulate_rows(x, g, m_shift, m_scale)
    up = jnp.dot(h.astype(bf16), wup_ref[...], preferred_element_type=f32)
    a = up[:, :D_FF]
    ac = conv(a)
    gl = _gelu_tanh(ac) * up[:, D_FF:]
    y = jnp.dot(gl.astype(bf16), wdown_ref[...], preferred_element_type=f32)
    return x + m_gate * y


def _ffn_prompt_body(with_attn, *refs):
    if with_attn:
        (x_ref, ot_ref, wo_ref, mod_ref, g_ref, wup_ref, cw_ref, wdown_ref, o_ref, st_ref, abuf) = refs
    else:
        (x_ref, mod_ref, g_ref, wup_ref, cw_ref, wdown_ref, o_ref, st_ref, abuf) = refs
    tm = x_ref.shape[1]
    x = x_ref[0]
    m = mod_ref[0]
    if with_attn:
        attn = lax.dot_general(ot_ref[0], wo_ref[...], (((0,), (0,)), ((), ())),
                               preferred_element_type=f32)
        x = x + m[2:3] * attn
    h = _modulate_rows(x, g_ref[...], m[3:4], m[4:5]).astype(bf16)
    cw = cw_ref[...]
    dot = functools.partial(jnp.dot, preferred_element_type=f32)
    _conv_start(abuf, pl.program_id(1) == 0)

    def up(c):
        c0 = c * CHUNK
        return dot(h, wup_ref[:, c0:c0 + CHUNK]), dot(h, wup_ref[:, D_FF + c0:D_FF + c0 + CHUNK])

    def down(c, a, v):
        c0 = c * CHUNK
        gl = _gelu_tanh(_conv_chunk(a, cw, abuf, c0)) * v
        return dot(gl.astype(bf16), wdown_ref[c0:c0 + CHUNK, :])

    y = _chunk_pipeline(D_FF // CHUNK, up, down)
    _conv_finish(abuf, st_ref, tm)
    o_ref[0] = x + m[5:6] * y


def _ffn_prompt(x, mod, g, w_up, cw, w_down, o_t=None, w_o=None):
    b, t, d = x.shape
    with_attn = o_t is not None
    xspec = pl.BlockSpec((1, TMF, d), lambda i, j: (i, j, 0))
    in_specs = [xspec]
    args = [x]
    if with_attn:
        in_specs += [pl.BlockSpec((1, d, TMF), lambda i, j: (i, 0, j)), _resident((d, d))]
        args += [o_t, w_o]
    in_specs += [
        pl.BlockSpec((1, 6, d), lambda i, j: (i, 0, 0)),
        pl.BlockSpec((1, d), lambda i, j: (0, 0)),
        _resident((d, 2 * D_FF)),
        pl.BlockSpec((3, D_FF), lambda i, j: (0, 0)),
        _resident((D_FF, d)),
    ]
    args += [mod, g, w_up, cw, w_down]
    return pl.pallas_call(
        functools.partial(_ffn_prompt_body, with_attn),
        out_shape=(jax.ShapeDtypeStruct((b, t, d), f32), jax.ShapeDtypeStruct((b, 2, D_FF), f32)),
        grid=(b, t // TMF),
        in_specs=in_specs,
        out_specs=(xspec, pl.BlockSpec((1, 2, D_FF), lambda i, j: (i, 0, 0))),
        scratch_shapes=[pltpu.VMEM((TMF + 8, D_FF), f32)],
        compiler_params=_cparams(("parallel", "arbitrary"), VMEM_LIMIT),
        name="ffn_attn_prompt" if with_attn else "ffn_prompt",
    )(*args)


def _ffn_sample_body(with_attn, *refs):
    if with_attn:
        (x_ref, o_ref_in, wo_ref, mod_ref, g_ref, p0_ref, p1_ref, wup_ref, cw_ref, wdown_ref,
         y_ref, a_ref) = refs
    else:
        (x_ref, mod_ref, g_ref, p0_ref, p1_ref, wup_ref, cw_ref, wdown_ref, y_ref, a_ref) = refs
    x = x_ref[...]
    if with_attn:
        attn = jnp.dot(o_ref_in[...].astype(bf16), wo_ref[...], preferred_element_type=f32)
        x = x + mod_ref[2] * attn
    cw = cw_ref[...]

    def conv(a):
        a_ref[...] = a
        return p0_ref[...] * cw[0:1] + p1_ref[...] * cw[1:2] + a * cw[2:3]

    y_ref[...] = _ffn_core(x, mod_ref[3], mod_ref[4], mod_ref[5], g_ref[...], wup_ref, wdown_ref, conv)


def _ffn_sample(x, mod, g, past0, past1, w_up, cw, w_down, o=None, w_o=None):
    n, d = x.shape
    with_attn = o is not None
    args = [x] + ([o, w_o] if with_attn else []) + [mod, g, past0, past1, w_up, cw, w_down]
    return pl.pallas_call(
        functools.partial(_ffn_sample_body, with_attn),
        out_shape=(jax.ShapeDtypeStruct((n, d), f32), jax.ShapeDtypeStruct((n, D_FF), f32)),
        compiler_params=_cparams(None, VMEM_LIMIT),
        name="ffn_attn_sample" if with_attn else "ffn_sample",
    )(*args)


def _aug_rows(f_row, negate_first):
    hi, mid, lo = (p.astype(f32) for p in _split3(f_row))
    tm = f_row.shape[1]
    row = lax.broadcasted_iota(jnp.int32, (AUG, tm), 0)
    one = jnp.ones((AUG, tm), f32)
    zero = jnp.zeros((AUG, tm), f32)
    if negate_first:
        parts = (-hi, -mid, -lo, one, one, one)
    else:
        parts = (one, one, one, hi, mid, lo)
    out = zero
    for r, p in enumerate(parts):
        out = jnp.where(row == r, jnp.broadcast_to(p, (AUG, tm)), out)
    return out


def _kvq_prompt_body(x_ref, modc_ref, pcol_ref, bf_ref, tri_ref, wk_ref, wv_ref, wf_ref, wq_ref,
                     kt_ref, vt_ref, lf_ref, qa_ref, ka_ref, vb_ref, carry):
    tm = x_ref.shape[1]
    xt = x_ref[0].T
    xn = xt * lax.rsqrt(jnp.mean(xt * xt, axis=0, keepdims=True) + EPS)
    pc = pcol_ref[...]
    mc = modc_ref[0]
    hk = (xn * pc[:, 0:1]).astype(bf16)
    hq = ((xn * pc[:, 1:2]) * (1.0 + mc[:, 1:2]) + mc[:, 0:1]).astype(bf16)
    kt = jnp.dot(wk_ref[...], hk, preferred_element_type=f32)
    vt = jnp.dot(wv_ref[...], hk, preferred_element_type=f32)
    ft = jnp.dot(wf_ref[...], hk, preferred_element_type=f32)
    qt = jnp.dot(wq_ref[...], hq, preferred_element_type=f32)

    logf = _log_sigmoid(ft + bf_ref[...])
    lf_ref[0] = logf

    @pl.when(pl.program_id(1) == 0)
    def _():
        carry[...] = jnp.zeros(carry.shape, f32)

    cum = _dot3(logf, tri_ref[...]) + carry[...]
    carry[...] = cum[:, tm - 1:tm]

    vt_ref[0] = vt
    vb_ref[0, :, 0:HEAD_DIM, :] = vt.astype(bf16).reshape(N_HEADS, HEAD_DIM, tm)
    ones_row = lax.broadcasted_iota(jnp.int32, (N_HEADS, AUG, tm), 1) == 0
    vb_ref[0, :, HEAD_DIM:HA, :] = ones_row.astype(bf16)
    for h in range(N_HEADS):
        r0, r1 = h * HEAD_DIM, (h + 1) * HEAD_DIM
        kh = kt[r0:r1]
        kh = kh * lax.rsqrt(jnp.mean(kh * kh, axis=0, keepdims=True) + EPS) * pc[r0:r1, 2:3]
        kt_ref[0, r0:r1, :] = kh
        qh = qt[r0:r1]
        qh = qh * lax.rsqrt(jnp.mean(qh * qh, axis=0, keepdims=True) + EPS) * pc[r0:r1, 3:4]
        f_row = cum[h:h + 1] * LOG2E
        ka_ref[0, h, 0:HEAD_DIM, :] = kh.astype(bf16)
        ka_ref[0, h, HEAD_DIM:HA, :] = _aug_rows(f_row, True).astype(bf16)
        qa_ref[0, h, 0:HEAD_DIM, :] = (qh * (ATTN_SCALE * LOG2E)).astype(bf16)
        qa_ref[0, h, HEAD_DIM:HA, :] = _aug_rows(f_row, False).astype(bf16)


def _kvq_prompt(x, modc, pcol, bf_col, tri, wk_t, wv_t, wf_t, wq_t):
    b, t, d = x.shape
    nt = t // TM
    out_shape = (
        jax.ShapeDtypeStruct((b, d, t), f32),
        jax.ShapeDtypeStruct((b, d, t), f32),
        jax.ShapeDtypeStruct((b, N_HEADS, t), f32),
        jax.ShapeDtypeStruct((b, N_HEADS, HA, t), bf16),
        jax.ShapeDtypeStruct((b, N_HEADS, HA, t), bf16),
        jax.ShapeDtypeStruct((b, N_HEADS, HA, t), bf16),
    )
    ct = pl.BlockSpec((1, d, TM), lambda i, j: (i, 0, j))
    return pl.pallas_call(
        _kvq_prompt_body,
        out_shape=out_shape,
        grid=(b, nt),
        in_specs=[
            pl.BlockSpec((1, TM, d), lambda i, j: (i, j, 0)),
            pl.BlockSpec((1, d, 2), lambda i, j: (i, 0, 0)),
            pl.BlockSpec((d, 4), lambda i, j: (0, 0)),
            pl.BlockSpec((N_HEADS, 1), lambda i, j: (0, 0)),
            pl.BlockSpec((TM, TM), lambda i, j: (0, 0)),
            _resident((d, d)), _resident((d, d)), _resident((N_HEADS, d)), _resident((d, d)),
        ],
        out_specs=(
            ct, ct,
            pl.BlockSpec((1, N_HEADS, TM), lambda i, j: (i, 0, j)),
            pl.BlockSpec((1, N_HEADS, HA, TM), lambda i, j: (i, 0, 0, j)),
            pl.BlockSpec((1, N_HEADS, HA, TM), lambda i, j: (i, 0, 0, j)),
            pl.BlockSpec((1, N_HEADS, HA, TM), lambda i, j: (i, 0, 0, j)),
        ),
        scratch_shapes=[pltpu.VMEM((N_HEADS, 1), f32)],
        compiler_params=_cparams(("parallel", "arbitrary"), VMEM_LIMIT),
        name="kvq_prompt",
    )(x, modc, pcol, bf_col, tri, wk_t, wv_t, wf_t, wq_t)


def _attn_prompt_body(qa_ref, ka_ref, vb_ref, o_ref):
    hg, _, t = qa_ref.shape[1:]
    tq = TM
    key = lax.broadcasted_iota(jnp.int32, (tq, tq), 0)
    qry = lax.broadcasted_iota(jnp.int32, (tq, tq), 1)
    causal = key <= qry
    contract0 = (((0,), (0,)), ((), ()))

    def scores(h, i):
        q0 = i * tq
        qa = qa_ref[0, h, :, q0:q0 + tq]
        s_d = lax.dot_general(ka_ref[0, h, :, q0:q0 + tq], qa, contract0,
                              preferred_element_type=f32)
        s_d = jnp.where(causal, s_d, -jnp.inf)
        m = jnp.max(s_d, axis=0, keepdims=True)
        s_o = None
        if i > 0:
            s_o = lax.dot_general(ka_ref[0, h, :, 0:q0], qa, contract0,
                                  preferred_element_type=f32)
            m = jnp.maximum(m, jnp.max(s_o, axis=0, keepdims=True))
        return s_d, s_o, m

    def weighted_values(h, i, s_d, s_o, m):
        q0 = i * tq
        p_d = jnp.exp2(s_d - m)
        acc = jnp.dot(vb_ref[0, h, :, q0:q0 + tq], p_d.astype(bf16), preferred_element_type=f32)
        if i > 0:
            p_o = jnp.exp2(s_o - m)
            acc = acc + jnp.dot(vb_ref[0, h, :, 0:q0], p_o.astype(bf16), preferred_element_type=f32)
        o_ref[0, h, :, q0:q0 + tq] = (acc[0:HEAD_DIM] / acc[HEAD_DIM:HEAD_DIM + 1]).astype(bf16)

    units = [(h, i) for h in range(hg) for i in range(t // tq)]
    ahead = 3
    pending = [scores(*u) for u in units[:ahead]]
    for n, unit in enumerate(units):
        if n + ahead < len(units):
            pending.append(scores(*units[n + ahead]))
        weighted_values(*unit, *pending.pop(0))


def _attn_prompt(qa, ka, vb):
    b, nh, _, t = qa.shape
    spec_a = pl.BlockSpec((1, ATT_HG, HA, t), lambda i, g: (i, g, 0, 0))
    spec_o = pl.BlockSpec((1, ATT_HG, HEAD_DIM, t), lambda i, g: (i, g, 0, 0))
    return pl.pallas_call(
        _attn_prompt_body,
        out_shape=jax.ShapeDtypeStruct((b, nh, HEAD_DIM, t), bf16),
        grid=(b, nh // ATT_HG),
        in_specs=[spec_a, spec_a, spec_a],
        out_specs=spec_o,
        compiler_params=_cparams(("parallel", "parallel"), VMEM_LIMIT),
        name="attn_prompt",
    )(qa, ka, vb)


def _head_rms_rows(x, gmat_ref, w_row):
    ms = _dot3(x * x, gmat_ref[...])
    return x * lax.rsqrt(ms + EPS) * w_row


def _kvq_sample_body(x_ref, mod_ref, prow_ref, bf_ref, gmat_ref, wkvf_ref, wq_ref,
                     k_ref, v_ref, lf_ref, q_ref):
    d = D_MODEL
    xn = _rms_rows(x_ref[...])
    pr = prow_ref[...]
    hk = (xn * pr[0:1]).astype(bf16)
    hq = ((xn * pr[1:2]) * (1.0 + mod_ref[1]) + mod_ref[0]).astype(bf16)
    kvf = jnp.dot(hk, wkvf_ref[...], preferred_element_type=f32)
    qr = jnp.dot(hq, wq_ref[...], preferred_element_type=f32)
    k_ref[...] = _head_rms_rows(kvf[:, :d], gmat_ref, pr[2:3])
    v_ref[...] = kvf[:, d:2 * d]
    lf_ref[...] = _log_sigmoid(kvf[:, 2 * d:] + bf_ref[...])
    q_ref[...] = _head_rms_rows(qr, gmat_ref, pr[3:4])


def _kvq_sample(x, mod, prow, bf_row, gmat, w_kvf_pad, w_q):
    n, d = x.shape
    return pl.pallas_call(
        _kvq_sample_body,
        out_shape=(jax.ShapeDtypeStruct((n, d), f32), jax.ShapeDtypeStruct((n, d), f32),
                   jax.ShapeDtypeStruct((n, 128), f32), jax.ShapeDtypeStruct((n, d), f32)),
        compiler_params=_cparams(None, VMEM_LIMIT),
        name="kvq_sample",
    )(x, mod, prow, bf_row, gmat, w_kvf_pad, w_q)


def _attn_sample_body(pt_ref, q_ref, kn_ref, vn_ref, fn_ref, rev_ref, *refs):
    g = DEC_PAGES
    k_refs, v_refs, lf_refs = refs[0:g], refs[g:2 * g], refs[2 * g:3 * g]
    o_ref = refs[3 * g]
    qbd_ref, m_ref, l_ref, suf_ref, acc_ref = refs[3 * g + 1:]
    step = pl.program_id(1)
    fn = fn_ref[0]
    hd_shape = (N_HEADS, D_MODEL)
    own = (lax.broadcasted_iota(jnp.int32, hd_shape, 1) // HEAD_DIM
           == lax.broadcasted_iota(jnp.int32, hd_shape, 0))

    @pl.when(step == 0)
    def _():
        qbd_ref[...] = jnp.where(own, jnp.broadcast_to(q_ref[0], hd_shape), 0.0)
        m_ref[...] = jnp.full(m_ref.shape, -jnp.inf, f32)
        l_ref[...] = jnp.zeros(l_ref.shape, f32)
        suf_ref[...] = jnp.zeros(suf_ref.shape, f32)
        acc_ref[...] = jnp.zeros(acc_ref.shape, f32)

    qbd = qbd_ref[...]
    qbd_b = qbd.astype(bf16)
    carry = suf_ref[...]
    scores = []
    for j in range(g):
        lf = lf_refs[j][0]
        suf = _dot3(lf, rev_ref[...]) + carry
        carry = carry + jnp.sum(lf, axis=1, keepdims=True)
        kb = k_refs[j][0].reshape(D_MODEL, PAGE).astype(bf16)
        s = jnp.dot(qbd_b, kb, preferred_element_type=f32) * ATTN_SCALE
        scores.append(s + fn + suf)
    suf_ref[...] = carry

    m_old = m_ref[...]
    m_new = m_old
    for s in scores:
        m_new = jnp.maximum(m_new, jnp.max(s, axis=1, keepdims=True))
    alpha = jnp.exp(m_old - m_new)
    probs = [jnp.exp(s - m_new) for s in scores]
    l_new = l_ref[...] * alpha
    for p in probs:
        l_new = l_new + jnp.sum(p, axis=1, keepdims=True)
    m_ref[...] = m_new
    l_ref[...] = l_new
    alpha_b = jnp.broadcast_to(alpha, (N_HEADS, PAGE))
    for h in range(N_HEADS):
        a = acc_ref[h] * alpha_b[h:h + 1, :]
        for j in range(g):
            a = a + v_refs[j][0, h] * probs[j][h:h + 1, :]
        acc_ref[h] = a

    @pl.when(step == pl.num_programs(1) - 1)
    def _():
        def spread(c):
            return jnp.sum(jnp.where(own, jnp.broadcast_to(c, hd_shape), 0.0), axis=0, keepdims=True)

        s_new = jnp.sum(qbd * jnp.broadcast_to(kn_ref[0], hd_shape), axis=1, keepdims=True) * ATTN_SCALE
        s_new = (s_new + fn) - fn
        m_fin = jnp.maximum(m_new, s_new)
        a_fin = jnp.exp(m_new - m_fin)
        p_new = jnp.exp(s_new - m_fin)
        l_fin = l_new * a_fin + p_new
        o_col = jnp.sum(acc_ref[...], axis=2, keepdims=True).reshape(D_MODEL, 1)
        o_row = jnp.broadcast_to(o_col, (D_MODEL, 128)).T[0:1, :]
        o_ref[0] = (o_row * spread(a_fin) + vn_ref[0] * spread(p_new)) / spread(l_fin)


def _attn_sample(page_table, q, k_new, v_new, logf_new, rev, cache_kt, cache_vt, cache_lft):
    n, n_pages = page_table.shape
    g = DEC_PAGES
    steps = n_pages // g
    row = pl.BlockSpec((1, 1, D_MODEL), lambda b, s, pt: (b, 0, 0))

    def page_spec(shape, j):
        nd = len(shape)
        return pl.BlockSpec(
            (1,) + shape,
            lambda b, s, pt: (pt[b, n_pages - 1 - (s * g + j)],) + (0,) * nd)

    in_specs = [row, row, row,
                pl.BlockSpec((1, N_HEADS, 1), lambda b, s, pt: (b, 0, 0)),
                pl.BlockSpec((PAGE, PAGE), lambda b, s, pt: (0, 0))]
    in_specs += [page_spec((N_HEADS, HEAD_DIM, PAGE), j) for j in range(g)]
    in_specs += [page_spec((N_HEADS, HEAD_DIM, PAGE), j) for j in range(g)]
    in_specs += [page_spec((N_HEADS, PAGE), j) for j in range(g)]
    grid_spec = pltpu.PrefetchScalarGridSpec(
        num_scalar_prefetch=1,
        grid=(n, steps),
        in_specs=in_specs,
        out_specs=row,
        scratch_shapes=[
            pltpu.VMEM((N_HEADS, D_MODEL), f32),
            pltpu.VMEM((N_HEADS, 1), f32),
            pltpu.VMEM((N_HEADS, 1), f32),
            pltpu.VMEM((N_HEADS, 1), f32),
            pltpu.VMEM((N_HEADS, HEAD_DIM, PAGE), f32),
        ],
    )
    return pl.pallas_call(
        _attn_sample_body,
        out_shape=jax.ShapeDtypeStruct((n, 1, D_MODEL), f32),
        grid_spec=grid_spec,
        compiler_params=_cparams(("parallel", "arbitrary"), VMEM_LIMIT),
        name="attn_sample",
    )(page_table, q[:, None, :], k_new[:, None, :], v_new[:, None, :], logf_new[:, :, None], rev,
      *([cache_kt] * g), *([cache_vt] * g), *([cache_lft] * g))


def kernel(x_prompt, x_sample, state_conv_a, state_ffn, cache_k, cache_v, cache_logf, page_table,
           c_prompt, c_sample, norm_mix, norm_ffn, w_ada, b_ada, w_in_a, conv_w_a, w_out_a, kv_norm,
           w_kvf, b_f, k_norm, w_q, q_norm, w_o, w_up, conv_w_ffn, w_down):
    d = D_MODEL
    nb, t, _ = x_prompt.shape
    ns = x_sample.shape[0]

    w_in = w_in_a[0].astype(bf16)
    w_out = w_out_a[0].astype(bf16)
    w_up_b = w_up.astype(bf16)
    w_down_b = w_down.astype(bf16)
    w_o_b = w_o[0].astype(bf16)
    w_q_b = w_q[0].astype(bf16)
    w_kvf_b = w_kvf.astype(bf16)
    wk_t = w_kvf_b[:, :d].T
    wv_t = w_kvf_b[:, d:2 * d].T
    wf_t = w_kvf_b[:, 2 * d:].T
    wq_t = w_q_b.T
    w_kvf_pad = jnp.pad(w_kvf_b, ((0, 0), (0, 128 - N_HEADS)))
    k_norm_d = jnp.tile(k_norm, N_HEADS)
    q_norm_d = jnp.tile(q_norm[0], N_HEADS)
    prow = jnp.stack([kv_norm, norm_mix[1], k_norm_d, q_norm_d])
    pcol = prow.T
    bf_row = jnp.pad(b_f, (0, 128 - N_HEADS))[None, :]
    bf_col = b_f[:, None]
    idx = jnp.arange(TM)
    tri = (idx[:, None] <= idx[None, :]).astype(bf16)
    pidx = jnp.arange(PAGE)
    rev = (pidx[:, None] > pidx[None, :]).astype(bf16)
    lane_head = jnp.arange(d) // HEAD_DIM
    gmat = ((lane_head[:, None] == lane_head[None, :]).astype(f32) / HEAD_DIM).astype(bf16)

    mod = _ada(jnp.concatenate([c_prompt, c_sample], axis=0), w_ada, b_ada)
    mod_p = mod[:, :nb].reshape(2, nb, 6, d)
    mod_s = mod[:, nb:].reshape(2, ns, 6, d).transpose(0, 2, 1, 3)
    modc_p = mod_p[1, :, 0:2].transpose(0, 2, 1)

    x1, conv_a_p = _mix_prompt(x_prompt, mod_p[0], norm_mix[0:1], w_in, conv_w_a[0], w_out)
    x2, ffn0_p = _ffn_prompt(x1, mod_p[0], norm_ffn[0:1], w_up_b[0], conv_w_ffn[0], w_down_b[0])
    kt_p, vt_p, lft_p, qa, ka, vb = _kvq_prompt(x2, modc_p, pcol, bf_col, tri, wk_t, wv_t, wf_t, wq_t)
    o_t = _attn_prompt(qa, ka, vb).reshape(nb, d, t)
    y_p, ffn1_p = _ffn_prompt(x2, mod_p[1], norm_ffn[1:2], w_up_b[1], conv_w_ffn[1], w_down_b[1],
                              o_t=o_t, w_o=w_o_b)

    xs = x_sample[:, 0, :]
    xs1, u_s = _mix_sample(xs, mod_s[0], norm_mix[0:1], state_conv_a[0, :, 0], state_conv_a[0, :, 1],
                           w_in, conv_w_a[0], w_out)
    xs2, a0_s = _ffn_sample(xs1, mod_s[0], norm_ffn[0:1], state_ffn[0, :, 0], state_ffn[0, :, 1],
                            w_up_b[0], conv_w_ffn[0], w_down_b[0])
    k_s, v_s, lf_s, q_s = _kvq_sample(xs2, mod_s[1], prow, bf_row, gmat, w_kvf_pad, w_q_b)
    lf_s = lf_s[:, :N_HEADS]
    cache_kt = cache_k.transpose(0, 2, 3, 1)
    cache_vt = cache_v.transpose(0, 2, 3, 1)
    cache_lft = cache_logf.transpose(0, 2, 1)
    o_s = _attn_sample(page_table, q_s, k_s, v_s, lf_s, rev, cache_kt, cache_vt, cache_lft)[:, 0, :]
    y_s, a1_s = _ffn_sample(xs2, mod_s[1], norm_ffn[1:2], state_ffn[1, :, 0], state_ffn[1, :, 1],
                            w_up_b[1], conv_w_ffn[1], w_down_b[1], o=o_s, w_o=w_o_b)

    conv_a_s = jnp.stack([state_conv_a[0, :, 1], u_s], axis=1)[None]
    ffn_p = jnp.stack([ffn0_p, ffn1_p])
    ffn_s = jnp.stack([jnp.stack([state_ffn[0, :, 1], a0_s], axis=1),
                       jnp.stack([state_ffn[1, :, 1], a1_s], axis=1)])
    k_p = kt_p.reshape(nb, N_HEADS, HEAD_DIM, t).transpose(0, 3, 1, 2)
    v_p = vt_p.reshape(nb, N_HEADS, HEAD_DIM, t).transpose(0, 3, 1, 2)
    logf_p = lft_p.transpose(0, 2, 1)
    return (y_p, y_s[:, None, :], conv_a_p[None], conv_a_s, ffn_p, ffn_s,
            k_p, v_p, logf_p,
            k_s.reshape(ns, 1, N_HEADS, HEAD_DIM), v_s.reshape(ns, 1, N_HEADS, HEAD_DIM),
            lf_s[:, None, :])
```

```python
import functools

import jax
import jax.numpy as jnp
from jax import lax
from jax.experimental import pallas as pl
from jax.experimental.pallas import tpu as pltpu

f32 = jnp.float32
bf16 = jnp.bfloat16

D_MODEL = 1024
N_HEADS = 16
HEAD_DIM = D_MODEL // N_HEADS
D_FF = 2816
PAGE = 128
EPS = 1e-6
ATTN_SCALE = HEAD_DIM ** -0.5
LOG2E = 1.4426950408889634

TM = 256
TMX = 512
TMF = 256
CHUNK = 256
AUG = 16
HA = HEAD_DIM + AUG
ATT_HG = 2
ADA_TN = 1024
VMEM_LIMIT = 56 * 1024 * 1024
DEC_ROWS = (4, 8, 4, 8, 8)


def _cparams(sem, vmem=None):
    return pltpu.CompilerParams(dimension_semantics=sem, vmem_limit_bytes=vmem)


def _resident(shape):
    nd = len(shape)
    return pl.BlockSpec(shape, lambda *_: (0,) * nd, pipeline_mode=pl.Buffered(1))


def _rms_rows(x):
    return x * lax.rsqrt(jnp.mean(x * x, axis=-1, keepdims=True) + EPS)


def _modulate_rows(x, g, shift, scale):
    return (_rms_rows(x) * g) * (1.0 + scale) + shift


def _gelu_tanh(x):
    c = 0.7978845608028654
    return x * (0.5 * (1.0 + jnp.tanh(c * (x + 0.044715 * (x * x * x)))))


def _log_sigmoid(x):
    return jnp.minimum(x, 0.0) - jnp.log1p(jnp.exp(-jnp.abs(x)))


def _split3(x):
    hi = x.astype(bf16)
    r1 = x - hi.astype(f32)
    mid = r1.astype(bf16)
    lo = (r1 - mid.astype(f32)).astype(bf16)
    return hi, mid, lo


def _dot3(x, m_bf16):
    hi, mid, lo = _split3(x)
    d = functools.partial(jnp.dot, preferred_element_type=f32)
    return (d(hi, m_bf16) + d(mid, m_bf16)) + d(lo, m_bf16)


def _ada_body(c_ref, w_ref, b_ref, o_ref):
    c = c_ref[...]
    s = c * (1.0 / (1.0 + jnp.exp(-c)))
    o_ref[0] = jnp.dot(s.astype(bf16), w_ref[0].astype(bf16), preferred_element_type=f32) + b_ref[0]


def _ada(c_all, w_ada, b_ada):
    n_layers, _, n_out = w_ada.shape
    rows = c_all.shape[0]
    return pl.pallas_call(
        _ada_body,
        out_shape=jax.ShapeDtypeStruct((n_layers, rows, n_out), f32),
        grid=(n_layers, n_out // ADA_TN),
        in_specs=[
            pl.BlockSpec((rows, D_MODEL), lambda l, j: (0, 0)),
            pl.BlockSpec((1, D_MODEL, ADA_TN), lambda l, j: (l, 0, j)),
            pl.BlockSpec((1, 1, ADA_TN), lambda l, j: (l, 0, j)),
        ],
        out_specs=pl.BlockSpec((1, rows, ADA_TN), lambda l, j: (l, 0, j)),
        compiler_params=_cparams(("parallel", "parallel")),
        name="ada",
    )(c_all, w_ada, b_ada.reshape(n_layers, 1, n_out))


def _own_lanes():
    shape = (N_HEADS, D_MODEL)
    return lax.broadcasted_iota(jnp.int32, shape, 1) // HEAD_DIM == lax.broadcasted_iota(jnp.int32, shape, 0)


def _block_diag(row):
    return jnp.where(_own_lanes(), jnp.broadcast_to(row, (N_HEADS, D_MODEL)), 0.0)


def _decode_init(o_ref, scratch):
    m_ref, l_ref, suf_ref, acc_ref = scratch
    m_ref[...] = jnp.full(m_ref.shape, -jnp.inf, f32)
    l_ref[...] = jnp.zeros(l_ref.shape, f32)
    suf_ref[...] = jnp.zeros(suf_ref.shape, f32)
    acc_ref[...] = jnp.zeros(acc_ref.shape, f32)
    o_ref[...] = jnp.zeros(o_ref.shape, f32)


def _decode_pages(ins, scratch, g):
    q_ref, _, _, fn_ref, rev_ref = ins[:5]
    k_refs, v_refs, lf_refs = ins[5:5 + g], ins[5 + g:5 + 2 * g], ins[5 + 2 * g:5 + 3 * g]
    m_ref, l_ref, suf_ref, acc_ref = scratch
    fn = fn_ref[0]
    qbd_b = _block_diag(q_ref[0]).astype(bf16)
    carry = suf_ref[...]
    scores = []
    for j in range(g):
        lf = lf_refs[j][0]
        suf = _dot3(lf, rev_ref[...]) + carry
        carry = carry + jnp.sum(lf, axis=1, keepdims=True)
        kb = k_refs[j][0].reshape(D_MODEL, PAGE).astype(bf16)
        s = jnp.dot(qbd_b, kb, preferred_element_type=f32) * ATTN_SCALE
        scores.append(s + fn + suf)
    suf_ref[...] = carry

    m_old = m_ref[...]
    m_new = m_old
    for s in scores:
        m_new = jnp.maximum(m_new, jnp.max(s, axis=1, keepdims=True))
    alpha = jnp.exp(m_old - m_new)
    probs = [jnp.exp(s - m_new) for s in scores]
    l_new = l_ref[...] * alpha
    for p in probs:
        l_new = l_new + jnp.sum(p, axis=1, keepdims=True)
    m_ref[...] = m_new
    l_ref[...] = l_new
    alpha_b = jnp.broadcast_to(alpha, (N_HEADS, PAGE))
    for h in range(N_HEADS):
        a = acc_ref[h] * alpha_b[h:h + 1, :]
        for j in range(g):
            a = a + v_refs[j][0, h] * probs[j][h:h + 1, :]
        acc_ref[h] = a


def _decode_finish(ins, o_ref, scratch):
    q_ref, kn_ref, vn_ref, fn_ref = ins[:4]
    m_ref, l_ref, _, acc_ref = scratch
    fn = fn_ref[0]
    own = _own_lanes()
    shape = (N_HEADS, D_MODEL)

    def spread(c):
        return jnp.sum(jnp.where(own, jnp.broadcast_to(c, shape), 0.0), axis=0, keepdims=True)

    qbd = _block_diag(q_ref[0])
    s_new = jnp.sum(qbd * jnp.broadcast_to(kn_ref[0], shape), axis=1, keepdims=True) * ATTN_SCALE
    s_new = (s_new + fn) - fn
    m_old = m_ref[...]
    m_fin = jnp.maximum(m_old, s_new)
    a_fin = jnp.exp(m_old - m_fin)
    p_new = jnp.exp(s_new - m_fin)
    l_fin = l_ref[...] * a_fin + p_new
    o_col = jnp.sum(acc_ref[...], axis=2, keepdims=True).reshape(D_MODEL, 1)
    o_row = jnp.broadcast_to(o_col, (D_MODEL, 128)).T[0:1, :]
    o_ref[0] = (o_row * spread(a_fin) + vn_ref[0] * spread(p_new)) / spread(l_fin)


def _host_call(name, pre, main, grid, in_specs, args, out_specs, out_shapes, scratch, dec):
    pt, q, kn, vn, fn, rev, ckt, cvt, clft = dec
    rows, n_pages = pt.shape
    nj = grid[1]
    steps = grid[0] * nj
    assert steps % rows == 0 and n_pages % (steps // rows) == 0
    spr = steps // rows
    g = n_pages // spr
    n_in, n_out, n_scr, n_dec = len(args), len(out_shapes), len(scratch), 5 + 3 * g

    def row_of(i, j):
        return (i * nj + j) // spr

    def page_spec(shape, jj):
        nd = len(shape)
        return pl.BlockSpec(
            (1,) + shape,
            lambda i, j, ptr: (ptr[row_of(i, j), n_pages - 1 - (((i * nj + j) % spr) * g + jj)],) + (0,) * nd)

    row_spec = pl.BlockSpec((1, 1, D_MODEL), lambda i, j, ptr: (row_of(i, j), 0, 0))
    dec_specs = [row_spec, row_spec, row_spec,
                 pl.BlockSpec((1, N_HEADS, 1), lambda i, j, ptr: (row_of(i, j), 0, 0)),
                 pl.BlockSpec((PAGE, PAGE), lambda i, j, ptr: (0, 0))]
    dec_specs += [page_spec((N_HEADS, HEAD_DIM, PAGE), jj) for jj in range(g)]
    dec_specs += [page_spec((N_HEADS, HEAD_DIM, PAGE), jj) for jj in range(g)]
    dec_specs += [page_spec((N_HEADS, PAGE), jj) for jj in range(g)]
    dec_args = [q[:, None, :], kn[:, None, :], vn[:, None, :], fn[:, :, None], rev]
    dec_args += [ckt] * g + [cvt] * g + [clft] * g
    dec_scratch = [
        pltpu.VMEM((N_HEADS, 1), f32),
        pltpu.VMEM((N_HEADS, 1), f32),
        pltpu.VMEM((N_HEADS, 1), f32),
        pltpu.VMEM((N_HEADS, HEAD_DIM, PAGE), f32),
    ]

    def body(pt_ref, *refs):
        host_in, dec_in = refs[:n_in], refs[n_in:n_in + n_dec]
        rest = refs[n_in + n_dec:]
        host_out, o_dec = rest[:n_out], rest[n_out]
        host_scr, dec_scr = rest[n_out + 1:n_out + 1 + n_scr], rest[n_out + 1 + n_scr:]
        host_refs = (*host_in, *host_out, *host_scr)
        pstep = (pl.program_id(0) * nj + pl.program_id(1)) % spr
        if pre is not None:
            pre(*host_refs)

        @pl.when(pstep == 0)
        def _():
            _decode_init(o_dec, dec_scr)

        _decode_pages(dec_in, dec_scr, g)
        main(*host_refs)

        @pl.when(pstep == spr - 1)
        def _():
            _decode_finish(dec_in, o_dec, dec_scr)

    grid_spec = pltpu.PrefetchScalarGridSpec(
        num_scalar_prefetch=1,
        grid=grid,
        in_specs=list(in_specs) + dec_specs,
        out_specs=(*out_specs, row_spec),
        scratch_shapes=list(scratch) + dec_scratch,
    )
    outs = pl.pallas_call(
        body,
        out_shape=(*out_shapes, jax.ShapeDtypeStruct((rows, 1, D_MODEL), f32)),
        grid_spec=grid_spec,
        compiler_params=_cparams(("arbitrary", "arbitrary"), VMEM_LIMIT),
        name=name,
    )(pt, *args, *dec_args)
    return outs[:-1], outs[-1][:, 0, :]


def _conv_start(buf):
    @pl.when(pl.program_id(1) == 0)
    def _():
        buf[0:8, :] = jnp.zeros((8, buf.shape[1]), f32)


def _conv_chunk(u, cw, buf, c0):
    tm, w = u.shape
    cols = slice(c0, c0 + w)
    buf[8:tm + 8, cols] = u
    return (buf[6:tm + 6, cols] * cw[0:1, cols] + buf[7:tm + 7, cols] * cw[1:2, cols]
            + u * cw[2:3, cols])


def _conv_finish(buf, st_ref, tm):
    st_ref[0] = buf[tm + 6:tm + 8, :]
    buf[0:8, :] = buf[tm:tm + 8, :]


def _chunk_pipeline(n, produce, consume):
    pending = [produce(0)]
    total = None
    for c in range(n):
        if c + 1 < n:
            pending.append(produce(c + 1))
        part = consume(c, *pending.pop(0))
        total = part if total is None else total + part
    return total


def _mix_prompt_pre(x_ref, mod_ref, g_ref, win_ref, cw_ref, wout_ref, o_ref, st_ref, ubuf):
    _conv_start(ubuf)


def _mix_prompt_main(x_ref, mod_ref, g_ref, win_ref, cw_ref, wout_ref, o_ref, st_ref, ubuf):
    d = D_MODEL
    tm = x_ref.shape[1]
    x = x_ref[0]
    m = mod_ref[0]
    h = _modulate_rows(x, g_ref[...], m[0:1], m[1:2]).astype(bf16)
    z = jnp.dot(h, win_ref[...], preferred_element_type=f32)
    uc = _conv_chunk(z[:, d:2 * d] * z[:, 2 * d:], cw_ref[...], ubuf, 0)
    _conv_finish(ubuf, st_ref, tm)
    y = jnp.dot((z[:, :d] * uc).astype(bf16), wout_ref[...], preferred_element_type=f32)
    o_ref[0] = x + m[2:3] * y


def _mix_prompt(x, mod, g, w_in, cw, w_out, dec):
    b, t, d = x.shape
    xspec = pl.BlockSpec((1, TMX, d), lambda i, j, *_: (i, j, 0))
    return _host_call(
        "mix_prompt", _mix_prompt_pre, _mix_prompt_main, (b, t // TMX),
        in_specs=[
            xspec,
            pl.BlockSpec((1, 6, d), lambda i, j, *_: (i, 0, 0)),
            pl.BlockSpec((1, d), lambda i, j, *_: (0, 0)),
            _resident((d, 3 * d)),
            pl.BlockSpec((3, d), lambda i, j, *_: (0, 0)),
            _resident((d, d)),
        ],
        args=[x, mod, g, w_in, cw, w_out],
        out_specs=(xspec, pl.BlockSpec((1, 2, d), lambda i, j, *_: (i, 0, 0))),
        out_shapes=(jax.ShapeDtypeStruct((b, t, d), f32), jax.ShapeDtypeStruct((b, 2, d), f32)),
        scratch=[pltpu.VMEM((TMX + 8, d), f32)],
        dec=dec)


def _mix_sample_body(x_ref, mod_ref, g_ref, p0_ref, p1_ref, win_ref, cw_ref, wout_ref, o_ref, u_ref):
    d = D_MODEL
    x = x_ref[...]
    h = _modulate_rows(x, g_ref[...], mod_ref[0], mod_ref[1])
    z = jnp.dot(h.astype(bf16), win_ref[...], preferred_element_type=f32)
    u = z[:, d:2 * d] * z[:, 2 * d:]
    cw = cw_ref[...]
    uc = p0_ref[...] * cw[0:1] + p1_ref[...] * cw[1:2] + u * cw[2:3]
    u_ref[...] = u
    y = jnp.dot((z[:, :d] * uc).astype(bf16), wout_ref[...], preferred_element_type=f32)
    o_ref[...] = x + mod_ref[2] * y


def _mix_sample(x, mod, g, past0, past1, w_in, cw, w_out):
    n, d = x.shape
    return pl.pallas_call(
        _mix_sample_body,
        out_shape=(jax.ShapeDtypeStruct((n, d), f32), jax.ShapeDtypeStruct((n, d), f32)),
        compiler_params=_cparams(None, VMEM_LIMIT),
        name="mix_sample",
    )(x, mod, g, past0, past1, w_in, cw, w_out)


def _ffn_prompt_pre(with_attn, *refs):
    _conv_start(refs[-1])


def _ffn_prompt_main(with_attn, *refs):
    if with_attn:
        (x_ref, ot_ref, wo_ref, mod_ref, g_ref, wup_ref, cw_ref, wdown_ref, o_ref, st_ref, abuf) = refs
    else:
        (x_ref, mod_ref, g_ref, wup_ref, cw_ref, wdown_ref, o_ref, st_ref, abuf) = refs
    tm = x_ref.shape[1]
    x = x_ref[0]
    m = mod_ref[0]
    if with_attn:
        attn = lax.dot_general(ot_ref[0], wo_ref[...], (((0,), (0,)), ((), ())),
                               preferred_element_type=f32)
        x = x + m[2:3] * attn
    h = _modulate_rows(x, g_ref[...], m[3:4], m[4:5]).astype(bf16)
    cw = cw_ref[...]
    dot = functools.partial(jnp.dot, preferred_element_type=f32)

    def up(c):
        c0 = c * CHUNK
        return dot(h, wup_ref[:, c0:c0 + CHUNK]), dot(h, wup_ref[:, D_FF + c0:D_FF + c0 + CHUNK])

    def down(c, a, v):
        c0 = c * CHUNK
        gl = _gelu_tanh(_conv_chunk(a, cw, abuf, c0)) * v
        return dot(gl.astype(bf16), wdown_ref[c0:c0 + CHUNK, :])

    y = _chunk_pipeline(D_FF // CHUNK, up, down)
    _conv_finish(abuf, st_ref, tm)
    o_ref[0] = x + m[5:6] * y


def _ffn_prompt(x, mod, g, w_up, cw, w_down, dec, o_t=None, w_o=None):
    b, t, d = x.shape
    with_attn = o_t is not None
    xspec = pl.BlockSpec((1, TMF, d), lambda i, j, *_: (i, j, 0))
    in_specs = [xspec]
    args = [x]
    if with_attn:
        in_specs += [pl.BlockSpec((1, d, TMF), lambda i, j, *_: (i, 0, j)), _resident((d, d))]
        args += [o_t, w_o]
    in_specs += [
        pl.BlockSpec((1, 6, d), lambda i, j, *_: (i, 0, 0)),
        pl.BlockSpec((1, d), lambda i, j, *_: (0, 0)),
        _resident((d, 2 * D_FF)),
        pl.BlockSpec((3, D_FF), lambda i, j, *_: (0, 0)),
        _resident((D_FF, d)),
    ]
    args += [mod, g, w_up, cw, w_down]
    return _host_call(
        "ffn_attn_prompt" if with_attn else "ffn_prompt",
        functools.partial(_ffn_prompt_pre, with_attn), functools.partial(_ffn_prompt_main, with_attn),
        (b, t // TMF), in_specs=in_specs, args=args,
        out_specs=(xspec, pl.BlockSpec((1, 2, D_FF), lambda i, j, *_: (i, 0, 0))),
        out_shapes=(jax.ShapeDtypeStruct((b, t, d), f32), jax.ShapeDtypeStruct((b, 2, D_FF), f32)),
        scratch=[pltpu.VMEM((TMF + 8, D_FF), f32)],
        dec=dec)


def _ffn_sample_body(with_attn, *refs):
    if with_attn:
        (x_ref, o_ref_in, wo_ref, mod_ref, g_ref, p0_ref, p1_ref, wup_ref, cw_ref, wdown_ref,
         y_ref, a_ref) = refs
    else:
        (x_ref, mod_ref, g_ref, p0_ref, p1_ref, wup_ref, cw_ref, wdown_ref, y_ref, a_ref) = refs
    x = x_ref[...]
    if with_attn:
        attn = jnp.dot(o_ref_in[...].astype(bf16), wo_ref[...], preferred_element_type=f32)
        x = x + mod_ref[2] * attn
    cw = cw_ref[...]
    h = _modulate_rows(x, g_ref[...], mod_ref[3], mod_ref[4])
    up = jnp.dot(h.astype(bf16), wup_ref[...], preferred_element_type=f32)
    a = up[:, :D_FF]
    a_ref[...] = a
    ac = p0_ref[...] * cw[0:1] + p1_ref[...] * cw[1:2] + a * cw[2:3]
    gl = _gelu_tanh(ac) * up[:, D_FF:]
    y = jnp.dot(gl.astype(bf16), wdown_ref[...], preferred_element_type=f32)
    y_ref[...] = x + mod_ref[5] * y


def _ffn_sample(x, mod, g, past0, past1, w_up, cw, w_down, o=None, w_o=None):
    n, d = x.shape
    with_attn = o is not None
    args = [x] + ([o, w_o] if with_attn else []) + [mod, g, past0, past1, w_up, cw, w_down]
    return pl.pallas_call(
        functools.partial(_ffn_sample_body, with_attn),
        out_shape=(jax.ShapeDtypeStruct((n, d), f32), jax.ShapeDtypeStruct((n, D_FF), f32)),
        compiler_params=_cparams(None, VMEM_LIMIT),
        name="ffn_attn_sample" if with_attn else "ffn_sample",
    )(*args)


def _aug_rows(f_row, negate_first):
    hi, mid, lo = (p.astype(f32) for p in _split3(f_row))
    tm = f_row.shape[1]
    row = lax.broadcasted_iota(jnp.int32, (AUG, tm), 0)
    one = jnp.ones((AUG, tm), f32)
    zero = jnp.zeros((AUG, tm), f32)
    if negate_first:
        parts = (-hi, -mid, -lo, one, one, one)
    else:
        parts = (one, one, one, hi, mid, lo)
    out = zero
    for r, p in enumerate(parts):
        out = jnp.where(row == r, jnp.broadcast_to(p, (AUG, tm)), out)
    return out


def _kvq_prompt_pre(*refs):
    carry = refs[-1]

    @pl.when(pl.program_id(1) == 0)
    def _():
        carry[...] = jnp.zeros(carry.shape, f32)


def _kvq_prompt_main(x_ref, modc_ref, pcol_ref, bf_ref, tri_ref, wk_ref, wv_ref, wf_ref, wq_ref,
                     kt_ref, vt_ref, lf_ref, qa_ref, ka_ref, vb_ref, carry):
    tm = x_ref.shape[1]
    xt = x_ref[0].T
    xn = xt * lax.rsqrt(jnp.mean(xt * xt, axis=0, keepdims=True) + EPS)
    pc = pcol_ref[...]
    mc = modc_ref[0]
    hk = (xn * pc[:, 0:1]).astype(bf16)
    hq = ((xn * pc[:, 1:2]) * (1.0 + mc[:, 1:2]) + mc[:, 0:1]).astype(bf16)
    kt = jnp.dot(wk_ref[...], hk, preferred_element_type=f32)
    vt = jnp.dot(wv_ref[...], hk, preferred_element_type=f32)
    ft = jnp.dot(wf_ref[...], hk, preferred_element_type=f32)
    qt = jnp.dot(wq_ref[...], hq, preferred_element_type=f32)

    logf = _log_sigmoid(ft + bf_ref[...])
    lf_ref[0] = logf
    cum = _dot3(logf, tri_ref[...]) + carry[...]
    carry[...] = cum[:, tm - 1:tm]

    vt_ref[0] = vt
    vb_ref[0, :, 0:HEAD_DIM, :] = vt.astype(bf16).reshape(N_HEADS, HEAD_DIM, tm)
    ones_row = lax.broadcasted_iota(jnp.int32, (N_HEADS, AUG, tm), 1) == 0
    vb_ref[0, :, HEAD_DIM:HA, :] = ones_row.astype(bf16)
    for h in range(N_HEADS):
        r0, r1 = h * HEAD_DIM, (h + 1) * HEAD_DIM
        kh = kt[r0:r1]
        kh = kh * lax.rsqrt(jnp.mean(kh * kh, axis=0, keepdims=True) + EPS) * pc[r0:r1, 2:3]
        kt_ref[0, r0:r1, :] = kh
        qh = qt[r0:r1]
        qh = qh * lax.rsqrt(jnp.mean(qh * qh, axis=0, keepdims=True) + EPS) * pc[r0:r1, 3:4]
        f_row = cum[h:h + 1] * LOG2E
        ka_ref[0, h, 0:HEAD_DIM, :] = kh.astype(bf16)
        ka_ref[0, h, HEAD_DIM:HA, :] = _aug_rows(f_row, True).astype(bf16)
        qa_ref[0, h, 0:HEAD_DIM, :] = (qh * (ATTN_SCALE * LOG2E)).astype(bf16)
        qa_ref[0, h, HEAD_DIM:HA, :] = _aug_rows(f_row, False).astype(bf16)


def _kvq_prompt(x, modc, pcol, bf_col, tri, wk_t, wv_t, wf_t, wq_t, dec):
    b, t, d = x.shape
    out_shapes = (
        jax.ShapeDtypeStruct((b, d, t), f32),
        jax.ShapeDtypeStruct((b, d, t), f32),
        jax.ShapeDtypeStruct((b, N_HEADS, t), f32),
        jax.ShapeDtypeStruct((b, N_HEADS, HA, t), bf16),
        jax.ShapeDtypeStruct((b, N_HEADS, HA, t), bf16),
        jax.ShapeDtypeStruct((b, N_HEADS, HA, t), bf16),
    )
    ct = pl.BlockSpec((1, d, TM), lambda i, j, *_: (i, 0, j))
    ha = pl.BlockSpec((1, N_HEADS, HA, TM), lambda i, j, *_: (i, 0, 0, j))
    return _host_call(
        "kvq_prompt", _kvq_prompt_pre, _kvq_prompt_main, (b, t // TM),
        in_specs=[
            pl.BlockSpec((1, TM, d), lambda i, j, *_: (i, j, 0)),
            pl.BlockSpec((1, d, 2), lambda i, j, *_: (i, 0, 0)),
            pl.BlockSpec((d, 4), lambda i, j, *_: (0, 0)),
            pl.BlockSpec((N_HEADS, 1), lambda i, j, *_: (0, 0)),
            pl.BlockSpec((TM, TM), lambda i, j, *_: (0, 0)),
            _resident((d, d)), _resident((d, d)), _resident((N_HEADS, d)), _resident((d, d)),
        ],
        args=[x, modc, pcol, bf_col, tri, wk_t, wv_t, wf_t, wq_t],
        out_specs=(ct, ct, pl.BlockSpec((1, N_HEADS, TM), lambda i, j, *_: (i, 0, j)), ha, ha, ha),
        out_shapes=out_shapes,
        scratch=[pltpu.VMEM((N_HEADS, 1), f32)],
        dec=dec)


def _attn_prompt_main(qa_ref, ka_ref, vb_ref, o_ref):
    hg, _, t = qa_ref.shape[1:]
    tq = TM
    key = lax.broadcasted_iota(jnp.int32, (tq, tq), 0)
    qry = lax.broadcasted_iota(jnp.int32, (tq, tq), 1)
    causal = key <= qry
    contract0 = (((0,), (0,)), ((), ()))

    def scores(h, i):
        q0 = i * tq
        qa = qa_ref[0, h, :, q0:q0 + tq]
        s_d = lax.dot_general(ka_ref[0, h, :, q0:q0 + tq], qa, contract0,
                              preferred_element_type=f32)
        s_d = jnp.where(causal, s_d, -jnp.inf)
        m = jnp.max(s_d, axis=0, keepdims=True)
        s_o = None
        if i > 0:
            s_o = lax.dot_general(ka_ref[0, h, :, 0:q0], qa, contract0,
                                  preferred_element_type=f32)
            m = jnp.maximum(m, jnp.max(s_o, axis=0, keepdims=True))
        return s_d, s_o, m

    def weighted_values(h, i, s_d, s_o, m):
        q0 = i * tq
        p_d = jnp.exp2(s_d - m)
        acc = jnp.dot(vb_ref[0, h, :, q0:q0 + tq], p_d.astype(bf16), preferred_element_type=f32)
        if i > 0:
            p_o = jnp.exp2(s_o - m)
            acc = acc + jnp.dot(vb_ref[0, h, :, 0:q0], p_o.astype(bf16), preferred_element_type=f32)
        o_ref[0, h, :, q0:q0 + tq] = (acc[0:HEAD_DIM] / acc[HEAD_DIM:HEAD_DIM + 1]).astype(bf16)

    units = [(h, i) for h in range(hg) for i in range(t // tq)]
    ahead = 3
    pending = [scores(*u) for u in units[:ahead]]
    for n, unit in enumerate(units):
        if n + ahead < len(units):
            pending.append(scores(*units[n + ahead]))
        weighted_values(*unit, *pending.pop(0))


def _attn_prompt(qa, ka, vb, dec):
    b, nh, _, t = qa.shape
    spec_a = pl.BlockSpec((1, ATT_HG, HA, t), lambda i, j, *_: (i, j, 0, 0))
    spec_o = pl.BlockSpec((1, ATT_HG, HEAD_DIM, t), lambda i, j, *_: (i, j, 0, 0))
    return _host_call(
        "attn_prompt", None, _attn_prompt_main, (b, nh // ATT_HG),
        in_specs=[spec_a, spec_a, spec_a], args=[qa, ka, vb],
        out_specs=(spec_o,), out_shapes=(jax.ShapeDtypeStruct((b, nh, HEAD_DIM, t), bf16),),
        scratch=[], dec=dec)


def _head_rms_rows(x, gmat_ref, w_row):
    ms = _dot3(x * x, gmat_ref[...])
    return x * lax.rsqrt(ms + EPS) * w_row


def _kvq_sample_body(x_ref, mod_ref, prow_ref, bf_ref, gmat_ref, wkvf_ref, wq_ref,
                     k_ref, v_ref, lf_ref, q_ref):
    d = D_MODEL
    xn = _rms_rows(x_ref[...])
    pr = prow_ref[...]
    hk = (xn * pr[0:1]).astype(bf16)
    hq = ((xn * pr[1:2]) * (1.0 + mod_ref[1]) + mod_ref[0]).astype(bf16)
    kvf = jnp.dot(hk, wkvf_ref[...], preferred_element_type=f32)
    qr = jnp.dot(hq, wq_ref[...], preferred_element_type=f32)
    k_ref[...] = _head_rms_rows(kvf[:, :d], gmat_ref, pr[2:3])
    v_ref[...] = kvf[:, d:2 * d]
    lf_ref[...] = _log_sigmoid(kvf[:, 2 * d:] + bf_ref[...])
    q_ref[...] = _head_rms_rows(qr, gmat_ref, pr[3:4])


def _kvq_sample(x, mod, prow, bf_row, gmat, w_kvf_pad, w_q):
    n, d = x.shape
    return pl.pallas_call(
        _kvq_sample_body,
        out_shape=(jax.ShapeDtypeStruct((n, d), f32), jax.ShapeDtypeStruct((n, d), f32),
                   jax.ShapeDtypeStruct((n, 128), f32), jax.ShapeDtypeStruct((n, d), f32)),
        compiler_params=_cparams(None, VMEM_LIMIT),
        name="kvq_sample",
    )(x, mod, prow, bf_row, gmat, w_kvf_pad, w_q)


def kernel(x_prompt, x_sample, state_conv_a, state_ffn, cache_k, cache_v, cache_logf, page_table,
           c_prompt, c_sample, norm_mix, norm_ffn, w_ada, b_ada, w_in_a, conv_w_a, w_out_a, kv_norm,
           w_kvf, b_f, k_norm, w_q, q_norm, w_o, w_up, conv_w_ffn, w_down):
    d = D_MODEL
    nb, t, _ = x_prompt.shape
    ns = x_sample.shape[0]
    assert sum(DEC_ROWS) == ns

    w_in = w_in_a[0].astype(bf16)
    w_out = w_out_a[0].astype(bf16)
    w_up_b = w_up.astype(bf16)
    w_down_b = w_down.astype(bf16)
    w_o_b = w_o[0].astype(bf16)
    w_q_b = w_q[0].astype(bf16)
    w_kvf_b = w_kvf.astype(bf16)
    wk_t = w_kvf_b[:, :d].T
    wv_t = w_kvf_b[:, d:2 * d].T
    wf_t = w_kvf_b[:, 2 * d:].T
    wq_t = w_q_b.T
    w_kvf_pad = jnp.pad(w_kvf_b, ((0, 0), (0, 128 - N_HEADS)))
    k_norm_d = jnp.tile(k_norm, N_HEADS)
    q_norm_d = jnp.tile(q_norm[0], N_HEADS)
    prow = jnp.stack([kv_norm, norm_mix[1], k_norm_d, q_norm_d])
    pcol = prow.T
    bf_row = jnp.pad(b_f, (0, 128 - N_HEADS))[None, :]
    bf_col = b_f[:, None]
    idx = jnp.arange(TM)
    tri = (idx[:, None] <= idx[None, :]).astype(bf16)
    pidx = jnp.arange(PAGE)
    rev = (pidx[:, None] > pidx[None, :]).astype(bf16)
    lane_head = jnp.arange(d) // HEAD_DIM
    gmat = ((lane_head[:, None] == lane_head[None, :]).astype(f32) / HEAD_DIM).astype(bf16)

    mod = _ada(jnp.concatenate([c_prompt, c_sample], axis=0), w_ada, b_ada)
    mod_p = mod[:, :nb].reshape(2, nb, 6, d)
    mod_s = mod[:, nb:].reshape(2, ns, 6, d).transpose(0, 2, 1, 3)
    modc_p = mod_p[1, :, 0:2].transpose(0, 2, 1)

    xs = x_sample[:, 0, :]
    xs1, u_s = _mix_sample(xs, mod_s[0], norm_mix[0:1], state_conv_a[0, :, 0], state_conv_a[0, :, 1],
                           w_in, conv_w_a[0], w_out)
    xs2, a0_s = _ffn_sample(xs1, mod_s[0], norm_ffn[0:1], state_ffn[0, :, 0], state_ffn[0, :, 1],
                            w_up_b[0], conv_w_ffn[0], w_down_b[0])
    k_s, v_s, lf_s, q_s = _kvq_sample(xs2, mod_s[1], prow, bf_row, gmat, w_kvf_pad, w_q_b)
    lf_s = lf_s[:, :N_HEADS]
    cache_kt = cache_k.transpose(0, 2, 3, 1)
    cache_vt = cache_v.transpose(0, 2, 3, 1)
    cache_lft = cache_logf.transpose(0, 2, 1)

    def dec_job(k):
        r0 = sum(DEC_ROWS[:k])
        rows = slice(r0, r0 + DEC_ROWS[k])
        return (page_table[rows], q_s[rows], k_s[rows], v_s[rows], lf_s[rows], rev,
                cache_kt, cache_vt, cache_lft)

    (x1, conv_a_p), o_s0 = _mix_prompt(x_prompt, mod_p[0], norm_mix[0:1], w_in, conv_w_a[0], w_out,
                                       dec_job(0))
    (x2, ffn0_p), o_s1 = _ffn_prompt(x1, mod_p[0], norm_ffn[0:1], w_up_b[0], conv_w_ffn[0], w_down_b[0],
                                     dec_job(1))
    (kt_p, vt_p, lft_p, qa, ka, vb), o_s2 = _kvq_prompt(x2, modc_p, pcol, bf_col, tri, wk_t, wv_t, wf_t,
                                                        wq_t, dec_job(2))
    (o_t,), o_s3 = _attn_prompt(qa, ka, vb, dec_job(3))
    (y_p, ffn1_p), o_s4 = _ffn_prompt(x2, mod_p[1], norm_ffn[1:2], w_up_b[1], conv_w_ffn[1], w_down_b[1],
                                      dec_job(4), o_t=o_t.reshape(nb, d, t), w_o=w_o_b)

    o_s = jnp.concatenate([o_s0, o_s1, o_s2, o_s3, o_s4], axis=0)
    y_s, a1_s = _ffn_sample(xs2, mod_s[1], norm_ffn[1:2], state_ffn[1, :, 0], state_ffn[1, :, 1],
                            w_up_b[1], conv_w_ffn[1], w_down_b[1], o=o_s, w_o=w_o_b)

    conv_a_s = jnp.stack([state_conv_a[0, :, 1], u_s], axis=1)[None]
    ffn_p = jnp.stack([ffn0_p, ffn1_p])
    ffn_s = jnp.stack([jnp.stack([state_ffn[0, :, 1], a0_s], axis=1),
                       jnp.stack([state_ffn[1, :, 1], a1_s], axis=1)])
    k_p = kt_p.reshape(nb, N_HEADS, HEAD_DIM, t).transpose(0, 3, 1, 2)
    v_p = vt_p.reshape(nb, N_HEADS, HEAD_DIM, t).transpose(0, 3, 1, 2)
    logf_p = lft_p.transpose(0, 2, 1)
    return (y_p, y_s[:, None, :], conv_a_p[None], conv_a_s, ffn_p, ffn_s,
            k_p, v_p, logf_p,
            k_s.reshape(ns, 1, N_HEADS, HEAD_DIM), v_s.reshape(ns, 1, N_HEADS, HEAD_DIM),
            lf_s[:, None, :])
```

```python
import functools

import jax
import jax.numpy as jnp
from jax import lax
from jax.experimental import pallas as pl
from jax.experimental.pallas import tpu as pltpu

f32 = jnp.float32
bf16 = jnp.bfloat16

D_MODEL = 1024
N_HEADS = 16
HEAD_DIM = D_MODEL // N_HEADS
D_FF = 2816
PAGE = 128
EPS = 1e-6
ATTN_SCALE = HEAD_DIM ** -0.5
LOG2E = 1.4426950408889634

TM = 256
TMX = 512
TMF = 256
CHUNK = 256
AUG = 16
HA = HEAD_DIM + AUG
ATT_HG = 2
ADA_TN = 1024
VMEM_LIMIT = 56 * 1024 * 1024
DEC_ROWS = (4, 8, 4, 8, 8)


def _cparams(sem, vmem=None):
    return pltpu.CompilerParams(dimension_semantics=sem, vmem_limit_bytes=vmem)


def _resident(shape):
    nd = len(shape)
    return pl.BlockSpec(shape, lambda *_: (0,) * nd, pipeline_mode=pl.Buffered(1))


def _rms_rows(x):
    return x * lax.rsqrt(jnp.mean(x * x, axis=-1, keepdims=True) + EPS)


def _modulate_rows(x, g, shift, scale):
    return (_rms_rows(x) * g) * (1.0 + scale) + shift


def _gelu_tanh(x):
    c = 0.7978845608028654
    return x * (0.5 * (1.0 + jnp.tanh(c * (x + 0.044715 * (x * x * x)))))


def _log_sigmoid(x):
    return jnp.minimum(x, 0.0) - jnp.log1p(jnp.exp(-jnp.abs(x)))


def _split3(x):
    hi = x.astype(bf16)
    r1 = x - hi.astype(f32)
    mid = r1.astype(bf16)
    lo = (r1 - mid.astype(f32)).astype(bf16)
    return hi, mid, lo


def _dot3(x, m_bf16):
    hi, mid, lo = _split3(x)
    d = functools.partial(jnp.dot, preferred_element_type=f32)
    return (d(hi, m_bf16) + d(mid, m_bf16)) + d(lo, m_bf16)


def _ada_body(c_ref, w_ref, b_ref, o_ref):
    c = c_ref[...]
    s = c * (1.0 / (1.0 + jnp.exp(-c)))
    o_ref[0] = jnp.dot(s.astype(bf16), w_ref[0].astype(bf16), preferred_element_type=f32) + b_ref[0]


def _ada(c_all, w_ada, b_ada):
    n_layers, _, n_out = w_ada.shape
    rows = c_all.shape[0]
    return pl.pallas_call(
        _ada_body,
        out_shape=jax.ShapeDtypeStruct((n_layers, rows, n_out), f32),
        grid=(n_layers, n_out // ADA_TN),
        in_specs=[
            pl.BlockSpec((rows, D_MODEL), lambda l, j: (0, 0)),
            pl.BlockSpec((1, D_MODEL, ADA_TN), lambda l, j: (l, 0, j)),
            pl.BlockSpec((1, 1, ADA_TN), lambda l, j: (l, 0, j)),
        ],
        out_specs=pl.BlockSpec((1, rows, ADA_TN), lambda l, j: (l, 0, j)),
        compiler_params=_cparams(("parallel", "parallel")),
        name="ada",
    )(c_all, w_ada, b_ada.reshape(n_layers, 1, n_out))


def _own_lanes():
    shape = (N_HEADS, D_MODEL)
    return lax.broadcasted_iota(jnp.int32, shape, 1) // HEAD_DIM == lax.broadcasted_iota(jnp.int32, shape, 0)


def _block_diag(row):
    return jnp.where(_own_lanes(), jnp.broadcast_to(row, (N_HEADS, D_MODEL)), 0.0)


def _decode_init(o_ref, scratch):
    m_ref, l_ref, suf_ref, acc_ref = scratch
    m_ref[...] = jnp.full(m_ref.shape, -jnp.inf, f32)
    l_ref[...] = jnp.zeros(l_ref.shape, f32)
    suf_ref[...] = jnp.zeros(suf_ref.shape, f32)
    acc_ref[...] = jnp.zeros(acc_ref.shape, f32)
    o_ref[...] = jnp.zeros(o_ref.shape, f32)


def _decode_scores(ins, scratch, g):
    q_ref, _, _, fn_ref, rev_ref = ins[:5]
    k_refs, lf_refs = ins[5:5 + g], ins[5 + 2 * g:5 + 3 * g]
    m_ref, l_ref, suf_ref, _ = scratch
    fn = fn_ref[0]
    qbd_b = _block_diag(q_ref[0]).astype(bf16)
    carry = suf_ref[...]
    lf_all = jnp.concatenate([lf_refs[j][0] for j in range(g)], axis=0)
    suf_all = _dot3(lf_all, rev_ref[...])
    tot_all = jnp.sum(lf_all, axis=1, keepdims=True)
    scores = []
    for j0 in range(0, g, 2):
        kb = jnp.concatenate([k_refs[j][0].reshape(D_MODEL, PAGE).astype(bf16) for j in (j0, j0 + 1)],
                             axis=1)
        s2 = jnp.dot(qbd_b, kb, preferred_element_type=f32) * ATTN_SCALE
        for j in (j0, j0 + 1):
            rows = slice(j * N_HEADS, (j + 1) * N_HEADS)
            suf = suf_all[rows] + carry
            carry = carry + tot_all[rows]
            scores.append(s2[:, (j - j0) * PAGE:(j - j0 + 1) * PAGE] + fn + suf)
    suf_ref[...] = carry

    m_old = m_ref[...]
    m_new = m_old
    for s in scores:
        m_new = jnp.maximum(m_new, jnp.max(s, axis=1, keepdims=True))
    alpha = jnp.exp(m_old - m_new)
    probs = [jnp.exp(s - m_new) for s in scores]
    l_new = l_ref[...] * alpha
    for p in probs:
        l_new = l_new + jnp.sum(p, axis=1, keepdims=True)
    m_ref[...] = m_new
    l_ref[...] = l_new
    return probs, alpha


def _decode_values(ins, scratch, g, probs, alpha):
    v_refs = ins[5 + g:5 + 2 * g]
    acc_ref = scratch[3]
    alpha_b = jnp.broadcast_to(alpha, (N_HEADS, PAGE))
    for h in range(N_HEADS):
        a = acc_ref[h] * alpha_b[h:h + 1, :]
        for j in range(g):
            a = a + v_refs[j][0, h] * probs[j][h:h + 1, :]
        acc_ref[h] = a


def _decode_finish(ins, o_ref, scratch):
    q_ref, kn_ref, vn_ref, fn_ref = ins[:4]
    m_ref, l_ref, _, acc_ref = scratch
    fn = fn_ref[0]
    own = _own_lanes()
    shape = (N_HEADS, D_MODEL)

    def spread(c):
        return jnp.sum(jnp.where(own, jnp.broadcast_to(c, shape), 0.0), axis=0, keepdims=True)

    qbd = _block_diag(q_ref[0])
    s_new = jnp.sum(qbd * jnp.broadcast_to(kn_ref[0], shape), axis=1, keepdims=True) * ATTN_SCALE
    s_new = (s_new + fn) - fn
    m_old = m_ref[...]
    m_fin = jnp.maximum(m_old, s_new)
    a_fin = jnp.exp(m_old - m_fin)
    p_new = jnp.exp(s_new - m_fin)
    l_fin = l_ref[...] * a_fin + p_new
    o_col = jnp.sum(acc_ref[...], axis=2, keepdims=True).reshape(D_MODEL, 1)
    o_row = jnp.broadcast_to(o_col, (D_MODEL, 128)).T[0:1, :]
    o_ref[0] = (o_row * spread(a_fin) + vn_ref[0] * spread(p_new)) / spread(l_fin)


def _host_call(name, pre, main, grid, in_specs, args, out_specs, out_shapes, scratch, dec):
    pt, q, kn, vn, fn, rev, ckt, cvt, clft = dec
    rows, n_pages = pt.shape
    nj = grid[1]
    steps = grid[0] * nj
    assert steps % rows == 0 and n_pages % (steps // rows) == 0
    spr = steps // rows
    g = n_pages // spr
    n_in, n_out, n_scr, n_dec = len(args), len(out_shapes), len(scratch), 5 + 3 * g

    def row_of(i, j):
        return (i * nj + j) // spr

    def page_spec(shape, jj):
        nd = len(shape)
        return pl.BlockSpec(
            (1,) + shape,
            lambda i, j, ptr: (ptr[row_of(i, j), n_pages - 1 - (((i * nj + j) % spr) * g + jj)],) + (0,) * nd)

    row_spec = pl.BlockSpec((1, 1, D_MODEL), lambda i, j, ptr: (row_of(i, j), 0, 0))
    dec_specs = [row_spec, row_spec, row_spec,
                 pl.BlockSpec((1, N_HEADS, 1), lambda i, j, ptr: (row_of(i, j), 0, 0)),
                 pl.BlockSpec((PAGE, PAGE), lambda i, j, ptr: (0, 0))]
    dec_specs += [page_spec((N_HEADS, HEAD_DIM, PAGE), jj) for jj in range(g)]
    dec_specs += [page_spec((N_HEADS, HEAD_DIM, PAGE), jj) for jj in range(g)]
    dec_specs += [page_spec((N_HEADS, PAGE), jj) for jj in range(g)]
    dec_args = [q[:, None, :], kn[:, None, :], vn[:, None, :], fn[:, :, None], rev]
    dec_args += [ckt] * g + [cvt] * g + [clft] * g
    dec_scratch = [
        pltpu.VMEM((N_HEADS, 1), f32),
        pltpu.VMEM((N_HEADS, 1), f32),
        pltpu.VMEM((N_HEADS, 1), f32),
        pltpu.VMEM((N_HEADS, HEAD_DIM, PAGE), f32),
    ]

    def body(pt_ref, *refs):
        host_in, dec_in = refs[:n_in], refs[n_in:n_in + n_dec]
        rest = refs[n_in + n_dec:]
        host_out, o_dec = rest[:n_out], rest[n_out]
        host_scr, dec_scr = rest[n_out + 1:n_out + 1 + n_scr], rest[n_out + 1 + n_scr:]
        host_refs = (*host_in, *host_out, *host_scr)
        pstep = (pl.program_id(0) * nj + pl.program_id(1)) % spr
        if pre is not None:
            pre(*host_refs)

        @pl.when(pstep == 0)
        def _():
            _decode_init(o_dec, dec_scr)

        probs, alpha = _decode_scores(dec_in, dec_scr, g)
        main(*host_refs)

        p_max = functools.reduce(jnp.maximum, probs)

        @pl.when(jnp.max(p_max) > 0.0)
        def _():
            _decode_values(dec_in, dec_scr, g, probs, alpha)

        @pl.when(pstep == spr - 1)
        def _():
            _decode_finish(dec_in, o_dec, dec_scr)

    grid_spec = pltpu.PrefetchScalarGridSpec(
        num_scalar_prefetch=1,
        grid=grid,
        in_specs=list(in_specs) + dec_specs,
        out_specs=(*out_specs, row_spec),
        scratch_shapes=list(scratch) + dec_scratch,
    )
    outs = pl.pallas_call(
        body,
        out_shape=(*out_shapes, jax.ShapeDtypeStruct((rows, 1, D_MODEL), f32)),
        grid_spec=grid_spec,
        compiler_params=_cparams(("arbitrary", "arbitrary"), VMEM_LIMIT),
        name=name,
    )(pt, *args, *dec_args)
    return outs[:-1], outs[-1][:, 0, :]


def _conv_start(buf):
    @pl.when(pl.program_id(1) == 0)
    def _():
        buf[0:8, :] = jnp.zeros((8, buf.shape[1]), f32)


def _conv_chunk(u, cw, buf, c0):
    tm, w = u.shape
    cols = slice(c0, c0 + w)
    buf[8:tm + 8, cols] = u
    return (buf[6:tm + 6, cols] * cw[0:1, cols] + buf[7:tm + 7, cols] * cw[1:2, cols]
            + u * cw[2:3, cols])


def _conv_finish(buf, st_ref, tm):
    st_ref[0] = buf[tm + 6:tm + 8, :]
    buf[0:8, :] = buf[tm:tm + 8, :]


def _chunk_pipeline(n, produce, consume):
    pending = [produce(0)]
    total = None
    for c in range(n):
        if c + 1 < n:
            pending.append(produce(c + 1))
        part = consume(c, *pending.pop(0))
        total = part if total is None else total + part
    return total


def _mix_prompt_pre(x_ref, mod_ref, g_ref, win_ref, cw_ref, wout_ref, o_ref, st_ref, ubuf):
    _conv_start(ubuf)


def _mix_prompt_main(x_ref, mod_ref, g_ref, win_ref, cw_ref, wout_ref, o_ref, st_ref, ubuf):
    d = D_MODEL
    tm = x_ref.shape[1]
    x = x_ref[0]
    m = mod_ref[0]
    h = _modulate_rows(x, g_ref[...], m[0:1], m[1:2]).astype(bf16)
    z = jnp.dot(h, win_ref[...], preferred_element_type=f32)
    uc = _conv_chunk(z[:, d:2 * d] * z[:, 2 * d:], cw_ref[...], ubuf, 0)
    _conv_finish(ubuf, st_ref, tm)
    y = jnp.dot((z[:, :d] * uc).astype(bf16), wout_ref[...], preferred_element_type=f32)
    o_ref[0] = x + m[2:3] * y


def _mix_prompt(x, mod, g, w_in, cw, w_out, dec):
    b, t, d = x.shape
    xspec = pl.BlockSpec((1, TMX, d), lambda i, j, *_: (i, j, 0))
    return _host_call(
        "mix_prompt", _mix_prompt_pre, _mix_prompt_main, (b, t // TMX),
        in_specs=[
            xspec,
            pl.BlockSpec((1, 6, d), lambda i, j, *_: (i, 0, 0)),
            pl.BlockSpec((1, d), lambda i, j, *_: (0, 0)),
            _resident((d, 3 * d)),
            pl.BlockSpec((3, d), lambda i, j, *_: (0, 0)),
            _resident((d, d)),
        ],
        args=[x, mod, g, w_in, cw, w_out],
        out_specs=(xspec, pl.BlockSpec((1, 2, d), lambda i, j, *_: (i, 0, 0))),
        out_shapes=(jax.ShapeDtypeStruct((b, t, d), f32), jax.ShapeDtypeStruct((b, 2, d), f32)),
        scratch=[pltpu.VMEM((TMX + 8, d), f32)],
        dec=dec)


def _mix_sample_body(x_ref, mod_ref, g_ref, p0_ref, p1_ref, win_ref, cw_ref, wout_ref, o_ref, u_ref):
    d = D_MODEL
    x = x_ref[...]
    h = _modulate_rows(x, g_ref[...], mod_ref[0], mod_ref[1])
    z = jnp.dot(h.astype(bf16), win_ref[...], preferred_element_type=f32)
    u = z[:, d:2 * d] * z[:, 2 * d:]
    cw = cw_ref[...]
    uc = p0_ref[...] * cw[0:1] + p1_ref[...] * cw[1:2] + u * cw[2:3]
    u_ref[...] = u
    y = jnp.dot((z[:, :d] * uc).astype(bf16), wout_ref[...], preferred_element_type=f32)
    o_ref[...] = x + mod_ref[2] * y


def _mix_sample(x, mod, g, past0, past1, w_in, cw, w_out):
    n, d = x.shape
    return pl.pallas_call(
        _mix_sample_body,
        out_shape=(jax.ShapeDtypeStruct((n, d), f32), jax.ShapeDtypeStruct((n, d), f32)),
        compiler_params=_cparams(None, VMEM_LIMIT),
        name="mix_sample",
    )(x, mod, g, past0, past1, w_in, cw, w_out)


def _ffn_prompt_pre(with_attn, *refs):
    _conv_start(refs[-1])


def _ffn_prompt_main(with_attn, *refs):
    if with_attn:
        (x_ref, ot_ref, wo_ref, mod_ref, g_ref, wup_ref, cw_ref, wdown_ref, o_ref, st_ref, abuf) = refs
    else:
        (x_ref, mod_ref, g_ref, wup_ref, cw_ref, wdown_ref, o_ref, st_ref, abuf) = refs
    tm = x_ref.shape[1]
    x = x_ref[0]
    m = mod_ref[0]
    if with_attn:
        attn = lax.dot_general(ot_ref[0], wo_ref[...], (((0,), (0,)), ((), ())),
                               preferred_element_type=f32)
        x = x + m[2:3] * attn
    h = _modulate_rows(x, g_ref[...], m[3:4], m[4:5]).astype(bf16)
    cw = cw_ref[...]
    dot = functools.partial(jnp.dot, preferred_element_type=f32)

    def up(c):
        c0 = c * CHUNK
        return dot(h, wup_ref[:, c0:c0 + CHUNK]), dot(h, wup_ref[:, D_FF + c0:D_FF + c0 + CHUNK])

    def down(c, a, v):
        c0 = c * CHUNK
        gl = _gelu_tanh(_conv_chunk(a, cw, abuf, c0)) * v
        return dot(gl.astype(bf16), wdown_ref[c0:c0 + CHUNK, :])

    y = _chunk_pipeline(D_FF // CHUNK, up, down)
    _conv_finish(abuf, st_ref, tm)
    o_ref[0] = x + m[5:6] * y


def _ffn_prompt(x, mod, g, w_up, cw, w_down, dec, o_t=None, w_o=None):
    b, t, d = x.shape
    with_attn = o_t is not None
    xspec = pl.BlockSpec((1, TMF, d), lambda i, j, *_: (i, j, 0))
    in_specs = [xspec]
    args = [x]
    if with_attn:
        in_specs += [pl.BlockSpec((1, d, TMF), lambda i, j, *_: (i, 0, j)), _resident((d, d))]
        args += [o_t, w_o]
    in_specs += [
        pl.BlockSpec((1, 6, d), lambda i, j, *_: (i, 0, 0)),
        pl.BlockSpec((1, d), lambda i, j, *_: (0, 0)),
        _resident((d, 2 * D_FF)),
        pl.BlockSpec((3, D_FF), lambda i, j, *_: (0, 0)),
        _resident((D_FF, d)),
    ]
    args += [mod, g, w_up, cw, w_down]
    return _host_call(
        "ffn_attn_prompt" if with_attn else "ffn_prompt",
        functools.partial(_ffn_prompt_pre, with_attn), functools.partial(_ffn_prompt_main, with_attn),
        (b, t // TMF), in_specs=in_specs, args=args,
        out_specs=(xspec, pl.BlockSpec((1, 2, D_FF), lambda i, j, *_: (i, 0, 0))),
        out_shapes=(jax.ShapeDtypeStruct((b, t, d), f32), jax.ShapeDtypeStruct((b, 2, D_FF), f32)),
        scratch=[pltpu.VMEM((TMF + 8, D_FF), f32)],
        dec=dec)


def _ffn_sample_body(with_attn, *refs):
    if with_attn:
        (x_ref, o_ref_in, wo_ref, mod_ref, g_ref, p0_ref, p1_ref, wup_ref, cw_ref, wdown_ref,
         y_ref, a_ref) = refs
    else:
        (x_ref, mod_ref, g_ref, p0_ref, p1_ref, wup_ref, cw_ref, wdown_ref, y_ref, a_ref) = refs
    x = x_ref[...]
    if with_attn:
        attn = jnp.dot(o_ref_in[...].astype(bf16), wo_ref[...], preferred_element_type=f32)
        x = x + mod_ref[2] * attn
    cw = cw_ref[...]
    h = _modulate_rows(x, g_ref[...], mod_ref[3], mod_ref[4])
    up = jnp.dot(h.astype(bf16), wup_ref[...], preferred_element_type=f32)
    a = up[:, :D_FF]
    a_ref[...] = a
    ac = p0_ref[...] * cw[0:1] + p1_ref[...] * cw[1:2] + a * cw[2:3]
    gl = _gelu_tanh(ac) * up[:, D_FF:]
    y = jnp.dot(gl.astype(bf16), wdown_ref[...], preferred_element_type=f32)
    y_ref[...] = x + mod_ref[5] * y


def _ffn_sample(x, mod, g, past0, past1, w_up, cw, w_down, o=None, w_o=None):
    n, d = x.shape
    with_attn = o is not None
    args = [x] + ([o, w_o] if with_attn else []) + [mod, g, past0, past1, w_up, cw, w_down]
    return pl.pallas_call(
        functools.partial(_ffn_sample_body, with_attn),
        out_shape=(jax.ShapeDtypeStruct((n, d), f32), jax.ShapeDtypeStruct((n, D_FF), f32)),
        compiler_params=_cparams(None, VMEM_LIMIT),
        name="ffn_attn_sample" if with_attn else "ffn_sample",
    )(*args)


def _aug_rows(f_row, negate_first):
    hi, mid, lo = (p.astype(f32) for p in _split3(f_row))
    tm = f_row.shape[1]
    row = lax.broadcasted_iota(jnp.int32, (AUG, tm), 0)
    one = jnp.ones((AUG, tm), f32)
    zero = jnp.zeros((AUG, tm), f32)
    if negate_first:
        parts = (-hi, -mid, -lo, one, one, one)
    else:
        parts = (one, one, one, hi, mid, lo)
    out = zero
    for r, p in enumerate(parts):
        out = jnp.where(row == r, jnp.broadcast_to(p, (AUG, tm)), out)
    return out


def _kvq_prompt_pre(*refs):
    carry = refs[-1]

    @pl.when(pl.program_id(1) == 0)
    def _():
        carry[...] = jnp.zeros(carry.shape, f32)


def _kvq_prompt_main(x_ref, modc_ref, pcol_ref, bf_ref, tri_ref, wk_ref, wv_ref, wf_ref, wq_ref,
                     kt_ref, vt_ref, lf_ref, qa_ref, ka_ref, vb_ref, carry):
    tm = x_ref.shape[1]
    xt = x_ref[0].T
    xn = xt * lax.rsqrt(jnp.mean(xt * xt, axis=0, keepdims=True) + EPS)
    pc = pcol_ref[...]
    mc = modc_ref[0]
    hk = (xn * pc[:, 0:1]).astype(bf16)
    hq = ((xn * pc[:, 1:2]) * (1.0 + mc[:, 1:2]) + mc[:, 0:1]).astype(bf16)
    kt = jnp.dot(wk_ref[...], hk, preferred_element_type=f32)
    vt = jnp.dot(wv_ref[...], hk, preferred_element_type=f32)
    ft = jnp.dot(wf_ref[...], hk, preferred_element_type=f32)
    qt = jnp.dot(wq_ref[...], hq, preferred_element_type=f32)

    logf = _log_sigmoid(ft + bf_ref[...])
    lf_ref[0] = logf
    cum = _dot3(logf, tri_ref[...]) + carry[...]
    carry[...] = cum[:, tm - 1:tm]

    vt_ref[0] = vt
    vb_ref[0, :, 0:HEAD_DIM, :] = vt.astype(bf16).reshape(N_HEADS, HEAD_DIM, tm)
    ones_row = lax.broadcasted_iota(jnp.int32, (N_HEADS, AUG, tm), 1) == 0
    vb_ref[0, :, HEAD_DIM:HA, :] = ones_row.astype(bf16)
    for h in range(N_HEADS):
        r0, r1 = h * HEAD_DIM, (h + 1) * HEAD_DIM
        kh = kt[r0:r1]
        kh = kh * lax.rsqrt(jnp.mean(kh * kh, axis=0, keepdims=True) + EPS) * pc[r0:r1, 2:3]
        kt_ref[0, r0:r1, :] = kh
        qh = qt[r0:r1]
        qh = qh * lax.rsqrt(jnp.mean(qh * qh, axis=0, keepdims=True) + EPS) * pc[r0:r1, 3:4]
        f_row = cum[h:h + 1] * LOG2E
        ka_ref[0, h, 0:HEAD_DIM, :] = kh.astype(bf16)
        ka_ref[0, h, HEAD_DIM:HA, :] = _aug_rows(f_row, True).astype(bf16)
        qa_ref[0, h, 0:HEAD_DIM, :] = (qh * (ATTN_SCALE * LOG2E)).astype(bf16)
        qa_ref[0, h, HEAD_DIM:HA, :] = _aug_rows(f_row, False).astype(bf16)


def _kvq_prompt(x, modc, pcol, bf_col, tri, wk_t, wv_t, wf_t, wq_t, dec):
    b, t, d = x.shape
    out_shapes = (
        jax.ShapeDtypeStruct((b, d, t), f32),
        jax.ShapeDtypeStruct((b, d, t), f32),
        jax.ShapeDtypeStruct((b, N_HEADS, t), f32),
        jax.ShapeDtypeStruct((b, N_HEADS, HA, t), bf16),
        jax.ShapeDtypeStruct((b, N_HEADS, HA, t), bf16),
        jax.ShapeDtypeStruct((b, N_HEADS, HA, t), bf16),
    )
    ct = pl.BlockSpec((1, d, TM), lambda i, j, *_: (i, 0, j))
    ha = pl.BlockSpec((1, N_HEADS, HA, TM), lambda i, j, *_: (i, 0, 0, j))
    return _host_call(
        "kvq_prompt", _kvq_prompt_pre, _kvq_prompt_main, (b, t // TM),
        in_specs=[
            pl.BlockSpec((1, TM, d), lambda i, j, *_: (i, j, 0)),
            pl.BlockSpec((1, d, 2), lambda i, j, *_: (i, 0, 0)),
            pl.BlockSpec((d, 4), lambda i, j, *_: (0, 0)),
            pl.BlockSpec((N_HEADS, 1), lambda i, j, *_: (0, 0)),
            pl.BlockSpec((TM, TM), lambda i, j, *_: (0, 0)),
            _resident((d, d)), _resident((d, d)), _resident((N_HEADS, d)), _resident((d, d)),
        ],
        args=[x, modc, pcol, bf_col, tri, wk_t, wv_t, wf_t, wq_t],
        out_specs=(ct, ct, pl.BlockSpec((1, N_HEADS, TM), lambda i, j, *_: (i, 0, j)), ha, ha, ha),
        out_shapes=out_shapes,
        scratch=[pltpu.VMEM((N_HEADS, 1), f32)],
        dec=dec)


def _attn_prompt_main(qa_ref, ka_ref, vb_ref, o_ref):
    hg, _, t = qa_ref.shape[1:]
    tq = TM
    key = lax.broadcasted_iota(jnp.int32, (tq, tq), 0)
    qry = lax.broadcasted_iota(jnp.int32, (tq, tq), 1)
    causal = key <= qry
    contract0 = (((0,), (0,)), ((), ()))

    def scores(h, i):
        q0 = i * tq
        qa = qa_ref[0, h, :, q0:q0 + tq]
        s_d = lax.dot_general(ka_ref[0, h, :, q0:q0 + tq], qa, contract0,
                              preferred_element_type=f32)
        s_d = jnp.where(causal, s_d, -jnp.inf)
        m = jnp.max(s_d, axis=0, keepdims=True)
        s_o = None
        if i > 0:
            s_o = lax.dot_general(ka_ref[0, h, :, 0:q0], qa, contract0,
                                  preferred_element_type=f32)
            m = jnp.maximum(m, jnp.max(s_o, axis=0, keepdims=True))
        return s_d, s_o, m

    def weighted_values(h, i, s_d, s_o, m):
        q0 = i * tq
        p_d = jnp.exp2(s_d - m)
        acc = jnp.dot(vb_ref[0, h, :, q0:q0 + tq], p_d.astype(bf16), preferred_element_type=f32)
        if i > 0:
            p_o = jnp.exp2(s_o - m)
            acc = acc + jnp.dot(vb_ref[0, h, :, 0:q0], p_o.astype(bf16), preferred_element_type=f32)
        o_ref[0, h, :, q0:q0 + tq] = (acc[0:HEAD_DIM] / acc[HEAD_DIM:HEAD_DIM + 1]).astype(bf16)

    units = [(h, i) for h in range(hg) for i in range(t // tq)]
    ahead = 3
    pending = [scores(*u) for u in units[:ahead]]
    for n, unit in enumerate(units):
        if n + ahead < len(units):
            pending.append(scores(*units[n + ahead]))
        weighted_values(*unit, *pending.pop(0))


def _attn_prompt(qa, ka, vb, dec):
    b, nh, _, t = qa.shape
    spec_a = pl.BlockSpec((1, ATT_HG, HA, t), lambda i, j, *_: (i, j, 0, 0))
    spec_o = pl.BlockSpec((1, ATT_HG, HEAD_DIM, t), lambda i, j, *_: (i, j, 0, 0))
    return _host_call(
        "attn_prompt", None, _attn_prompt_main, (b, nh // ATT_HG),
        in_specs=[spec_a, spec_a, spec_a], args=[qa, ka, vb],
        out_specs=(spec_o,), out_shapes=(jax.ShapeDtypeStruct((b, nh, HEAD_DIM, t), bf16),),
        scratch=[], dec=dec)


def _head_rms_rows(x, gmat_ref, w_row):
    ms = _dot3(x * x, gmat_ref[...])
    return x * lax.rsqrt(ms + EPS) * w_row


def _kvq_sample_body(x_ref, mod_ref, prow_ref, bf_ref, gmat_ref, wkvf_ref, wq_ref,
                     k_ref, v_ref, lf_ref, q_ref):
    d = D_MODEL
    xn = _rms_rows(x_ref[...])
    pr = prow_ref[...]
    hk = (xn * pr[0:1]).astype(bf16)
    hq = ((xn * pr[1:2]) * (1.0 + mod_ref[1]) + mod_ref[0]).astype(bf16)
    kvf = jnp.dot(hk, wkvf_ref[...], preferred_element_type=f32)
    qr = jnp.dot(hq, wq_ref[...], preferred_element_type=f32)
    k_ref[...] = _head_rms_rows(kvf[:, :d], gmat_ref, pr[2:3])
    v_ref[...] = kvf[:, d:2 * d]
    lf_ref[...] = _log_sigmoid(kvf[:, 2 * d:] + bf_ref[...])
    q_ref[...] = _head_rms_rows(qr, gmat_ref, pr[3:4])


def _kvq_sample(x, mod, prow, bf_row, gmat, w_kvf_pad, w_q):
    n, d = x.shape
    return pl.pallas_call(
        _kvq_sample_body,
        out_shape=(jax.ShapeDtypeStruct((n, d), f32), jax.ShapeDtypeStruct((n, d), f32),
                   jax.ShapeDtypeStruct((n, 128), f32), jax.ShapeDtypeStruct((n, d), f32)),
        compiler_params=_cparams(None, VMEM_LIMIT),
        name="kvq_sample",
    )(x, mod, prow, bf_row, gmat, w_kvf_pad, w_q)


def kernel(x_prompt, x_sample, state_conv_a, state_ffn, cache_k, cache_v, cache_logf, page_table,
           c_prompt, c_sample, norm_mix, norm_ffn, w_ada, b_ada, w_in_a, conv_w_a, w_out_a, kv_norm,
           w_kvf, b_f, k_norm, w_q, q_norm, w_o, w_up, conv_w_ffn, w_down):
    d = D_MODEL
    nb, t, _ = x_prompt.shape
    ns = x_sample.shape[0]
    assert sum(DEC_ROWS) == ns

    w_in = w_in_a[0].astype(bf16)
    w_out = w_out_a[0].astype(bf16)
    w_up_b = w_up.astype(bf16)
    w_down_b = w_down.astype(bf16)
    w_o_b = w_o[0].astype(bf16)
    w_q_b = w_q[0].astype(bf16)
    w_kvf_b = w_kvf.astype(bf16)
    wk_t = w_kvf_b[:, :d].T
    wv_t = w_kvf_b[:, d:2 * d].T
    wf_t = w_kvf_b[:, 2 * d:].T
    wq_t = w_q_b.T
    w_kvf_pad = jnp.pad(w_kvf_b, ((0, 0), (0, 128 - N_HEADS)))
    k_norm_d = jnp.tile(k_norm, N_HEADS)
    q_norm_d = jnp.tile(q_norm[0], N_HEADS)
    prow = jnp.stack([kv_norm, norm_mix[1], k_norm_d, q_norm_d])
    pcol = prow.T
    bf_row = jnp.pad(b_f, (0, 128 - N_HEADS))[None, :]
    bf_col = b_f[:, None]
    idx = jnp.arange(TM)
    tri = (idx[:, None] <= idx[None, :]).astype(bf16)
    pidx = jnp.arange(PAGE)
    rev = (pidx[:, None] > pidx[None, :]).astype(bf16)
    lane_head = jnp.arange(d) // HEAD_DIM
    gmat = ((lane_head[:, None] == lane_head[None, :]).astype(f32) / HEAD_DIM).astype(bf16)

    mod = _ada(jnp.concatenate([c_prompt, c_sample], axis=0), w_ada, b_ada)
    mod_p = mod[:, :nb].reshape(2, nb, 6, d)
    mod_s = mod[:, nb:].reshape(2, ns, 6, d).transpose(0, 2, 1, 3)
    modc_p = mod_p[1, :, 0:2].transpose(0, 2, 1)

    xs = x_sample[:, 0, :]
    xs1, u_s = _mix_sample(xs, mod_s[0], norm_mix[0:1], state_conv_a[0, :, 0], state_conv_a[0, :, 1],
                           w_in, conv_w_a[0], w_out)
    xs2, a0_s = _ffn_sample(xs1, mod_s[0], norm_ffn[0:1], state_ffn[0, :, 0], state_ffn[0, :, 1],
                            w_up_b[0], conv_w_ffn[0], w_down_b[0])
    k_s, v_s, lf_s, q_s = _kvq_sample(xs2, mod_s[1], prow, bf_row, gmat, w_kvf_pad, w_q_b)
    lf_s = lf_s[:, :N_HEADS]
    cache_kt = cache_k.transpose(0, 2, 3, 1)
    cache_vt = cache_v.transpose(0, 2, 3, 1)
    cache_lft = cache_logf.transpose(0, 2, 1)

    def dec_job(k):
        r0 = sum(DEC_ROWS[:k])
        rows = slice(r0, r0 + DEC_ROWS[k])
        return (page_table[rows], q_s[rows], k_s[rows], v_s[rows], lf_s[rows], rev,
                cache_kt, cache_vt, cache_lft)

    (x1, conv_a_p), o_s0 = _mix_prompt(x_prompt, mod_p[0], norm_mix[0:1], w_in, conv_w_a[0], w_out,
                                       dec_job(0))
    (x2, ffn0_p), o_s1 = _ffn_prompt(x1, mod_p[0], norm_ffn[0:1], w_up_b[0], conv_w_ffn[0], w_down_b[0],
                                     dec_job(1))
    (kt_p, vt_p, lft_p, qa, ka, vb), o_s2 = _kvq_prompt(x2, modc_p, pcol, bf_col, tri, wk_t, wv_t, wf_t,
                                                        wq_t, dec_job(2))
    (o_t,), o_s3 = _attn_prompt(qa, ka, vb, dec_job(3))
    (y_p, ffn1_p), o_s4 = _ffn_prompt(x2, mod_p[1], norm_ffn[1:2], w_up_b[1], conv_w_ffn[1], w_down_b[1],
                                      dec_job(4), o_t=o_t.reshape(nb, d, t), w_o=w_o_b)

    o_s = jnp.concatenate([o_s0, o_s1, o_s2, o_s3, o_s4], axis=0)
    y_s, a1_s = _ffn_sample(xs2, mod_s[1], norm_ffn[1:2], state_ffn[1, :, 0], state_ffn[1, :, 1],
                            w_up_b[1], conv_w_ffn[1], w_down_b[1], o=o_s, w_o=w_o_b)

    conv_a_s = jnp.stack([state_conv_a[0, :, 1], u_s], axis=1)[None]
    ffn_p = jnp.stack([ffn0_p, ffn1_p])
    ffn_s = jnp.stack([jnp.stack([state_ffn[0, :, 1], a0_s], axis=1),
                       jnp.stack([state_ffn[1, :, 1], a1_s], axis=1)])
    k_p = kt_p.reshape(nb, N_HEADS, HEAD_DIM, t).transpose(0, 3, 1, 2)
    v_p = vt_p.reshape(nb, N_HEADS, HEAD_DIM, t).transpose(0, 3, 1, 2)
    logf_p = lft_p.transpose(0, 2, 1)
    return (y_p, y_s[:, None, :], conv_a_p[None], conv_a_s, ffn_p, ffn_s,
            k_p, v_p, logf_p,
            k_s.reshape(ns, 1, N_HEADS, HEAD_DIM), v_s.reshape(ns, 1, N_HEADS, HEAD_DIM),
            lf_s[:, None, :])
```

```python
import functools

import jax
import jax.numpy as jnp
from jax import lax
from jax.experimental import pallas as pl
from jax.experimental.pallas import tpu as pltpu

f32 = jnp.float32
bf16 = jnp.bfloat16

D_MODEL = 1024
N_HEADS = 16
HEAD_DIM = D_MODEL // N_HEADS
D_FF = 2816
PAGE = 128
EPS = 1e-6
ATTN_SCALE = HEAD_DIM ** -0.5
LOG2E = 1.4426950408889634

TM = 256
TMX = 512
TMF = 256
CHUNK = 256
AUG = 16
HA = HEAD_DIM + AUG
ATT_HG = 2
ADA_TN = 1024
VMEM_LIMIT = 56 * 1024 * 1024
DEC_ROWS = (4, 8, 4, 8, 8)


def _cparams(sem, vmem=None):
    return pltpu.CompilerParams(dimension_semantics=sem, vmem_limit_bytes=vmem)


def _resident(shape):
    nd = len(shape)
    return pl.BlockSpec(shape, lambda *_: (0,) * nd, pipeline_mode=pl.Buffered(1))


def _rms_rows(x):
    return x * lax.rsqrt(jnp.mean(x * x, axis=-1, keepdims=True) + EPS)


def _modulate_rows(x, g, shift, scale):
    return (_rms_rows(x) * g) * (1.0 + scale) + shift


def _gelu_tanh(x):
    c = 0.7978845608028654
    return x * (0.5 * (1.0 + jnp.tanh(c * (x + 0.044715 * (x * x * x)))))


def _log_sigmoid(x):
    return jnp.minimum(x, 0.0) - jnp.log1p(jnp.exp(-jnp.abs(x)))


def _split3(x):
    hi = x.astype(bf16)
    r1 = x - hi.astype(f32)
    mid = r1.astype(bf16)
    lo = (r1 - mid.astype(f32)).astype(bf16)
    return hi, mid, lo


def _dot3(x, m_bf16):
    hi, mid, lo = _split3(x)
    d = functools.partial(jnp.dot, preferred_element_type=f32)
    return (d(hi, m_bf16) + d(mid, m_bf16)) + d(lo, m_bf16)


def _ada_body(c_ref, w_ref, b_ref, o_ref):
    c = c_ref[...]
    s = c * (1.0 / (1.0 + jnp.exp(-c)))
    o_ref[0] = jnp.dot(s.astype(bf16), w_ref[0].astype(bf16), preferred_element_type=f32) + b_ref[0]


def _ada(c_all, w_ada, b_ada):
    n_layers, _, n_out = w_ada.shape
    rows = c_all.shape[0]
    return pl.pallas_call(
        _ada_body,
        out_shape=jax.ShapeDtypeStruct((n_layers, rows, n_out), f32),
        grid=(n_layers, n_out // ADA_TN),
        in_specs=[
            pl.BlockSpec((rows, D_MODEL), lambda l, j: (0, 0)),
            pl.BlockSpec((1, D_MODEL, ADA_TN), lambda l, j: (l, 0, j)),
            pl.BlockSpec((1, 1, ADA_TN), lambda l, j: (l, 0, j)),
        ],
        out_specs=pl.BlockSpec((1, rows, ADA_TN), lambda l, j: (l, 0, j)),
        compiler_params=_cparams(("parallel", "parallel")),
        name="ada",
    )(c_all, w_ada, b_ada.reshape(n_layers, 1, n_out))


def _own_lanes():
    shape = (N_HEADS, D_MODEL)
    return lax.broadcasted_iota(jnp.int32, shape, 1) // HEAD_DIM == lax.broadcasted_iota(jnp.int32, shape, 0)


def _block_diag(row):
    return jnp.where(_own_lanes(), jnp.broadcast_to(row, (N_HEADS, D_MODEL)), 0.0)


def _decode_init(o_ref, scratch):
    m_ref, l_ref, suf_ref, acc_ref = scratch
    m_ref[...] = jnp.full(m_ref.shape, -jnp.inf, f32)
    l_ref[...] = jnp.zeros(l_ref.shape, f32)
    suf_ref[...] = jnp.zeros(suf_ref.shape, f32)
    acc_ref[...] = jnp.zeros(acc_ref.shape, f32)
    o_ref[...] = jnp.zeros(o_ref.shape, f32)


def _decode_scores(q_ref, fn_ref, rev_ref, k_page, lf_page, scratch, g):
    m_ref, l_ref, suf_ref, _ = scratch
    fn = fn_ref[0]
    qbd_b = _block_diag(q_ref[0]).astype(bf16)
    carry = suf_ref[...]
    lf_all = jnp.concatenate([lf_page(j) for j in range(g)], axis=0)
    suf_all = _dot3(lf_all, rev_ref[...])
    tot_all = jnp.sum(lf_all, axis=1, keepdims=True)
    scores = []
    for j0 in range(0, g, 2):
        kb = jnp.concatenate([k_page(j).reshape(D_MODEL, PAGE).astype(bf16) for j in (j0, j0 + 1)],
                             axis=1)
        s2 = jnp.dot(qbd_b, kb, preferred_element_type=f32) * ATTN_SCALE
        for j in (j0, j0 + 1):
            rows = slice(j * N_HEADS, (j + 1) * N_HEADS)
            suf = suf_all[rows] + carry
            carry = carry + tot_all[rows]
            scores.append(s2[:, (j - j0) * PAGE:(j - j0 + 1) * PAGE] + fn + suf)
    suf_ref[...] = carry

    m_old = m_ref[...]
    m_new = m_old
    for s in scores:
        m_new = jnp.maximum(m_new, jnp.max(s, axis=1, keepdims=True))
    alpha = jnp.exp(m_old - m_new)
    probs = [jnp.exp(s - m_new) for s in scores]
    l_new = l_ref[...] * alpha
    for p in probs:
        l_new = l_new + jnp.sum(p, axis=1, keepdims=True)
    m_ref[...] = m_new
    l_ref[...] = l_new
    return probs, alpha


def _decode_values(v_page, scratch, g, probs, alpha):
    acc_ref = scratch[3]
    alpha_b = jnp.broadcast_to(alpha, (N_HEADS, PAGE))
    for h in range(N_HEADS):
        a = acc_ref[h] * alpha_b[h:h + 1, :]
        for j in range(g):
            a = a + v_page(j, h) * probs[j][h:h + 1, :]
        acc_ref[h] = a


def _decode_finish(ins, o_ref, scratch):
    q_ref, kn_ref, vn_ref, fn_ref = ins[:4]
    m_ref, l_ref, _, acc_ref = scratch
    fn = fn_ref[0]
    own = _own_lanes()
    shape = (N_HEADS, D_MODEL)

    def spread(c):
        return jnp.sum(jnp.where(own, jnp.broadcast_to(c, shape), 0.0), axis=0, keepdims=True)

    qbd = _block_diag(q_ref[0])
    s_new = jnp.sum(qbd * jnp.broadcast_to(kn_ref[0], shape), axis=1, keepdims=True) * ATTN_SCALE
    s_new = (s_new + fn) - fn
    m_old = m_ref[...]
    m_fin = jnp.maximum(m_old, s_new)
    a_fin = jnp.exp(m_old - m_fin)
    p_new = jnp.exp(s_new - m_fin)
    l_fin = l_ref[...] * a_fin + p_new
    o_col = jnp.sum(acc_ref[...], axis=2, keepdims=True).reshape(D_MODEL, 1)
    o_row = jnp.broadcast_to(o_col, (D_MODEL, 128)).T[0:1, :]
    o_ref[0] = (o_row * spread(a_fin) + vn_ref[0] * spread(p_new)) / spread(l_fin)


def _host_call(name, pre, main, grid, in_specs, args, out_specs, out_shapes, scratch, dec):
    pt, q, kn, vn, fn, rev, ckt, cvt, clft = dec
    rows, n_pages = pt.shape
    nj = grid[1]
    steps = grid[0] * nj
    assert steps % rows == 0 and n_pages % (steps // rows) == 0
    spr = steps // rows
    g = n_pages // spr
    n_in, n_out, n_scr, n_dec = len(args), len(out_shapes), len(scratch), 8

    def row_of(i, j):
        return (i * nj + j) // spr

    row_spec = pl.BlockSpec((1, 1, D_MODEL), lambda i, j, ptr: (row_of(i, j), 0, 0))
    hbm_spec = pl.BlockSpec(memory_space=pl.ANY)
    dec_specs = [row_spec, row_spec, row_spec,
                 pl.BlockSpec((1, N_HEADS, 1), lambda i, j, ptr: (row_of(i, j), 0, 0)),
                 pl.BlockSpec((PAGE, PAGE), lambda i, j, ptr: (0, 0)),
                 hbm_spec, hbm_spec, hbm_spec]
    dec_args = [q[:, None, :], kn[:, None, :], vn[:, None, :], fn[:, :, None], rev, ckt, cvt, clft]
    dec_scratch = [
        pltpu.VMEM((N_HEADS, 1), f32),
        pltpu.VMEM((N_HEADS, 1), f32),
        pltpu.VMEM((N_HEADS, 1), f32),
        pltpu.VMEM((N_HEADS, HEAD_DIM, PAGE), f32),
        pltpu.VMEM((2 * g, N_HEADS, HEAD_DIM, PAGE), f32),
        pltpu.VMEM((2 * g, N_HEADS, HEAD_DIM, PAGE), f32),
        pltpu.VMEM((2 * g, N_HEADS, PAGE), f32),
        pltpu.SemaphoreType.DMA((2, 3, g)),
    ]

    def body(pt_ref, *refs):
        host_in, dec_in = refs[:n_in], refs[n_in:n_in + n_dec]
        rest = refs[n_in + n_dec:]
        host_out, o_dec = rest[:n_out], rest[n_out]
        host_scr, dec_rest = rest[n_out + 1:n_out + 1 + n_scr], rest[n_out + 1 + n_scr:]
        dec_scr, (kbuf, vbuf, lfbuf, sem) = dec_rest[:4], dec_rest[4:]
        q_ref, _, _, fn_ref, rev_ref, ckt_ref, cvt_ref, clft_ref = dec_in
        host_refs = (*host_in, *host_out, *host_scr)
        lin = pl.program_id(0) * nj + pl.program_id(1)
        pstep = lin % spr
        slot = lin % 2

        def page_copies(step, to_slot):
            row = step // spr
            first_page = n_pages - 1 - (step % spr) * g
            copies = []
            for jj in range(g):
                phys = pt_ref[row, first_page - jj]
                dst = to_slot * g + jj
                copies.append(pltpu.make_async_copy(ckt_ref.at[phys], kbuf.at[dst], sem.at[to_slot, 0, jj]))
                copies.append(pltpu.make_async_copy(cvt_ref.at[phys], vbuf.at[dst], sem.at[to_slot, 1, jj]))
                copies.append(pltpu.make_async_copy(clft_ref.at[phys], lfbuf.at[dst], sem.at[to_slot, 2, jj]))
            return copies

        @pl.when(lin == 0)
        def _():
            for c in page_copies(0, 0):
                c.start()

        @pl.when(lin + 1 < steps)
        def _():
            for c in page_copies(lin + 1, 1 - slot):
                c.start()

        if pre is not None:
            pre(*host_refs)

        @pl.when(pstep == 0)
        def _():
            _decode_init(o_dec, dec_scr)

        for c in page_copies(lin, slot):
            c.wait()

        probs, alpha = _decode_scores(q_ref, fn_ref, rev_ref, lambda j: kbuf[slot * g + j],
                                      lambda j: lfbuf[slot * g + j], dec_scr, g)
        main(*host_refs)

        p_max = functools.reduce(jnp.maximum, probs)

        @pl.when(jnp.max(p_max) > 0.0)
        def _():
            _decode_values(lambda j, h: vbuf[slot * g + j, h], dec_scr, g, probs, alpha)

        @pl.when(pstep == spr - 1)
        def _():
            _decode_finish(dec_in, o_dec, dec_scr)

    grid_spec = pltpu.PrefetchScalarGridSpec(
        num_scalar_prefetch=1,
        grid=grid,
        in_specs=list(in_specs) + dec_specs,
        out_specs=(*out_specs, row_spec),
        scratch_shapes=list(scratch) + dec_scratch,
    )
    outs = pl.pallas_call(
        body,
        out_shape=(*out_shapes, jax.ShapeDtypeStruct((rows, 1, D_MODEL), f32)),
        grid_spec=grid_spec,
        compiler_params=_cparams(("arbitrary", "arbitrary"), VMEM_LIMIT),
        name=name,
    )(pt, *args, *dec_args)
    return outs[:-1], outs[-1][:, 0, :]


def _conv_start(buf):
    @pl.when(pl.program_id(1) == 0)
    def _():
        buf[0:8, :] = jnp.zeros((8, buf.shape[1]), f32)


def _conv_chunk(u, cw, buf, c0):
    tm, w = u.shape
    cols = slice(c0, c0 + w)
    buf[8:tm + 8, cols] = u
    return (buf[6:tm + 6, cols] * cw[0:1, cols] + buf[7:tm + 7, cols] * cw[1:2, cols]
            + u * cw[2:3, cols])


def _conv_finish(buf, st_ref, tm):
    st_ref[0] = buf[tm + 6:tm + 8, :]
    buf[0:8, :] = buf[tm:tm + 8, :]


def _chunk_pipeline(n, produce, consume):
    pending = [produce(0)]
    total = None
    for c in range(n):
        if c + 1 < n:
            pending.append(produce(c + 1))
        part = consume(c, *pending.pop(0))
        total = part if total is None else total + part
    return total


def _mix_prompt_pre(x_ref, mod_ref, g_ref, win_ref, cw_ref, wout_ref, o_ref, st_ref, ubuf):
    _conv_start(ubuf)


def _mix_prompt_main(x_ref, mod_ref, g_ref, win_ref, cw_ref, wout_ref, o_ref, st_ref, ubuf):
    d = D_MODEL
    tm = x_ref.shape[1]
    x = x_ref[0]
    m = mod_ref[0]
    h = _modulate_rows(x, g_ref[...], m[0:1], m[1:2]).astype(bf16)
    z = jnp.dot(h, win_ref[...], preferred_element_type=f32)
    uc = _conv_chunk(z[:, d:2 * d] * z[:, 2 * d:], cw_ref[...], ubuf, 0)
    _conv_finish(ubuf, st_ref, tm)
    y = jnp.dot((z[:, :d] * uc).astype(bf16), wout_ref[...], preferred_element_type=f32)
    o_ref[0] = x + m[2:3] * y


def _mix_prompt(x, mod, g, w_in, cw, w_out, dec):
    b, t, d = x.shape
    xspec = pl.BlockSpec((1, TMX, d), lambda i, j, *_: (i, j, 0))
    return _host_call(
        "mix_prompt", _mix_prompt_pre, _mix_prompt_main, (b, t // TMX),
        in_specs=[
            xspec,
            pl.BlockSpec((1, 6, d), lambda i, j, *_: (i, 0, 0)),
            pl.BlockSpec((1, d), lambda i, j, *_: (0, 0)),
            _resident((d, 3 * d)),
            pl.BlockSpec((3, d), lambda i, j, *_: (0, 0)),
            _resident((d, d)),
        ],
        args=[x, mod, g, w_in, cw, w_out],
        out_specs=(xspec, pl.BlockSpec((1, 2, d), lambda i, j, *_: (i, 0, 0))),
        out_shapes=(jax.ShapeDtypeStruct((b, t, d), f32), jax.ShapeDtypeStruct((b, 2, d), f32)),
        scratch=[pltpu.VMEM((TMX + 8, d), f32)],
        dec=dec)


def _mix_sample_body(x_ref, mod_ref, g_ref, p0_ref, p1_ref, win_ref, cw_ref, wout_ref, o_ref, u_ref):
    d = D_MODEL
    x = x_ref[...]
    h = _modulate_rows(x, g_ref[...], mod_ref[0], mod_ref[1])
    z = jnp.dot(h.astype(bf16), win_ref[...], preferred_element_type=f32)
    u = z[:, d:2 * d] * z[:, 2 * d:]
    cw = cw_ref[...]
    uc = p0_ref[...] * cw[0:1] + p1_ref[...] * cw[1:2] + u * cw[2:3]
    u_ref[...] = u
    y = jnp.dot((z[:, :d] * uc).astype(bf16), wout_ref[...], preferred_element_type=f32)
    o_ref[...] = x + mod_ref[2] * y


def _mix_sample(x, mod, g, past0, past1, w_in, cw, w_out):
    n, d = x.shape
    return pl.pallas_call(
        _mix_sample_body,
        out_shape=(jax.ShapeDtypeStruct((n, d), f32), jax.ShapeDtypeStruct((n, d), f32)),
        compiler_params=_cparams(None, VMEM_LIMIT),
        name="mix_sample",
    )(x, mod, g, past0, past1, w_in, cw, w_out)


def _ffn_prompt_pre(with_attn, *refs):
    _conv_start(refs[-1])


def _ffn_prompt_main(with_attn, *refs):
    if with_attn:
        (x_ref, ot_ref, wo_ref, mod_ref, g_ref, wup_ref, cw_ref, wdown_ref, o_ref, st_ref, abuf) = refs
    else:
        (x_ref, mod_ref, g_ref, wup_ref, cw_ref, wdown_ref, o_ref, st_ref, abuf) = refs
    tm = x_ref.shape[1]
    x = x_ref[0]
    m = mod_ref[0]
    if with_attn:
        attn = lax.dot_general(ot_ref[0], wo_ref[...], (((0,), (0,)), ((), ())),
                               preferred_element_type=f32)
        x = x + m[2:3] * attn
    h = _modulate_rows(x, g_ref[...], m[3:4], m[4:5]).astype(bf16)
    cw = cw_ref[...]
    dot = functools.partial(jnp.dot, preferred_element_type=f32)

    def up(c):
        c0 = c * CHUNK
        return dot(h, wup_ref[:, c0:c0 + CHUNK]), dot(h, wup_ref[:, D_FF + c0:D_FF + c0 + CHUNK])

    def down(c, a, v):
        c0 = c * CHUNK
        gl = _gelu_tanh(_conv_chunk(a, cw, abuf, c0)) * v
        return dot(gl.astype(bf16), wdown_ref[c0:c0 + CHUNK, :])

    y = _chunk_pipeline(D_FF // CHUNK, up, down)
    _conv_finish(abuf, st_ref, tm)
    o_ref[0] = x + m[5:6] * y


def _ffn_prompt(x, mod, g, w_up, cw, w_down, dec, o_t=None, w_o=None):
    b, t, d = x.shape
    with_attn = o_t is not None
    xspec = pl.BlockSpec((1, TMF, d), lambda i, j, *_: (i, j, 0))
    in_specs = [xspec]
    args = [x]
    if with_attn:
        in_specs += [pl.BlockSpec((1, d, TMF), lambda i, j, *_: (i, 0, j)), _resident((d, d))]
        args += [o_t, w_o]
    in_specs += [
        pl.BlockSpec((1, 6, d), lambda i, j, *_: (i, 0, 0)),
        pl.BlockSpec((1, d), lambda i, j, *_: (0, 0)),
        _resident((d, 2 * D_FF)),
        pl.BlockSpec((3, D_FF), lambda i, j, *_: (0, 0)),
        _resident((D_FF, d)),
    ]
    args += [mod, g, w_up, cw, w_down]
    return _host_call(
        "ffn_attn_prompt" if with_attn else "ffn_prompt",
        functools.partial(_ffn_prompt_pre, with_attn), functools.partial(_ffn_prompt_main, with_attn),
        (b, t // TMF), in_specs=in_specs, args=args,
        out_specs=(xspec, pl.BlockSpec((1, 2, D_FF), lambda i, j, *_: (i, 0, 0))),
        out_shapes=(jax.ShapeDtypeStruct((b, t, d), f32), jax.ShapeDtypeStruct((b, 2, D_FF), f32)),
        scratch=[pltpu.VMEM((TMF + 8, D_FF), f32)],
        dec=dec)


def _ffn_sample_body(with_attn, *refs):
    if with_attn:
        (x_ref, o_ref_in, wo_ref, mod_ref, g_ref, p0_ref, p1_ref, wup_ref, cw_ref, wdown_ref,
         y_ref, a_ref) = refs
    else:
        (x_ref, mod_ref, g_ref, p0_ref, p1_ref, wup_ref, cw_ref, wdown_ref, y_ref, a_ref) = refs
    x = x_ref[...]
    if with_attn:
        attn = jnp.dot(o_ref_in[...].astype(bf16), wo_ref[...], preferred_element_type=f32)
        x = x + mod_ref[2] * attn
    cw = cw_ref[...]
    h = _modulate_rows(x, g_ref[...], mod_ref[3], mod_ref[4])
    up = jnp.dot(h.astype(bf16), wup_ref[...], preferred_element_type=f32)
    a = up[:, :D_FF]
    a_ref[...] = a
    ac = p0_ref[...] * cw[0:1] + p1_ref[...] * cw[1:2] + a * cw[2:3]
    gl = _gelu_tanh(ac) * up[:, D_FF:]
    y = jnp.dot(gl.astype(bf16), wdown_ref[...], preferred_element_type=f32)
    y_ref[...] = x + mod_ref[5] * y


def _ffn_sample(x, mod, g, past0, past1, w_up, cw, w_down, o=None, w_o=None):
    n, d = x.shape
    with_attn = o is not None
    args = [x] + ([o, w_o] if with_attn else []) + [mod, g, past0, past1, w_up, cw, w_down]
    return pl.pallas_call(
        functools.partial(_ffn_sample_body, with_attn),
        out_shape=(jax.ShapeDtypeStruct((n, d), f32), jax.ShapeDtypeStruct((n, D_FF), f32)),
        compiler_params=_cparams(None, VMEM_LIMIT),
        name="ffn_attn_sample" if with_attn else "ffn_sample",
    )(*args)


def _aug_rows(f_row, negate_first):
    hi, mid, lo = (p.astype(f32) for p in _split3(f_row))
    tm = f_row.shape[1]
    row = lax.broadcasted_iota(jnp.int32, (AUG, tm), 0)
    one = jnp.ones((AUG, tm), f32)
    zero = jnp.zeros((AUG, tm), f32)
    if negate_first:
        parts = (-hi, -mid, -lo, one, one, one)
    else:
        parts = (one, one, one, hi, mid, lo)
    out = zero
    for r, p in enumerate(parts):
        out = jnp.where(row == r, jnp.broadcast_to(p, (AUG, tm)), out)
    return out


def _kvq_prompt_pre(*refs):
    carry = refs[-1]

    @pl.when(pl.program_id(1) == 0)
    def _():
        carry[...] = jnp.zeros(carry.shape, f32)


def _kvq_prompt_main(x_ref, modc_ref, pcol_ref, bf_ref, tri_ref, wk_ref, wv_ref, wf_ref, wq_ref,
                     kt_ref, vt_ref, lf_ref, qa_ref, ka_ref, vb_ref, carry):
    tm = x_ref.shape[1]
    xt = x_ref[0].T
    xn = xt * lax.rsqrt(jnp.mean(xt * xt, axis=0, keepdims=True) + EPS)
    pc = pcol_ref[...]
    mc = modc_ref[0]
    hk = (xn * pc[:, 0:1]).astype(bf16)
    hq = ((xn * pc[:, 1:2]) * (1.0 + mc[:, 1:2]) + mc[:, 0:1]).astype(bf16)
    kt = jnp.dot(wk_ref[...], hk, preferred_element_type=f32)
    vt = jnp.dot(wv_ref[...], hk, preferred_element_type=f32)
    ft = jnp.dot(wf_ref[...], hk, preferred_element_type=f32)
    qt = jnp.dot(wq_ref[...], hq, preferred_element_type=f32)

    logf = _log_sigmoid(ft + bf_ref[...])
    lf_ref[0] = logf
    cum = _dot3(logf, tri_ref[...]) + carry[...]
    carry[...] = cum[:, tm - 1:tm]

    vt_ref[0] = vt
    vb_ref[0, :, 0:HEAD_DIM, :] = vt.astype(bf16).reshape(N_HEADS, HEAD_DIM, tm)
    ones_row = lax.broadcasted_iota(jnp.int32, (N_HEADS, AUG, tm), 1) == 0
    vb_ref[0, :, HEAD_DIM:HA, :] = ones_row.astype(bf16)
    for h in range(N_HEADS):
        r0, r1 = h * HEAD_DIM, (h + 1) * HEAD_DIM
        kh = kt[r0:r1]
        kh = kh * lax.rsqrt(jnp.mean(kh * kh, axis=0, keepdims=True) + EPS) * pc[r0:r1, 2:3]
        kt_ref[0, r0:r1, :] = kh
        qh = qt[r0:r1]
        qh = qh * lax.rsqrt(jnp.mean(qh * qh, axis=0, keepdims=True) + EPS) * pc[r0:r1, 3:4]
        f_row = cum[h:h + 1] * LOG2E
        ka_ref[0, h, 0:HEAD_DIM, :] = kh.astype(bf16)
        ka_ref[0, h, HEAD_DIM:HA, :] = _aug_rows(f_row, True).astype(bf16)
        qa_ref[0, h, 0:HEAD_DIM, :] = (qh * (ATTN_SCALE * LOG2E)).astype(bf16)
        qa_ref[0, h, HEAD_DIM:HA, :] = _aug_rows(f_row, False).astype(bf16)


def _kvq_prompt(x, modc, pcol, bf_col, tri, wk_t, wv_t, wf_t, wq_t, dec):
    b, t, d = x.shape
    out_shapes = (
        jax.ShapeDtypeStruct((b, d, t), f32),
        jax.ShapeDtypeStruct((b, d, t), f32),
        jax.ShapeDtypeStruct((b, N_HEADS, t), f32),
        jax.ShapeDtypeStruct((b, N_HEADS, HA, t), bf16),
        jax.ShapeDtypeStruct((b, N_HEADS, HA, t), bf16),
        jax.ShapeDtypeStruct((b, N_HEADS, HA, t), bf16),
    )
    ct = pl.BlockSpec((1, d, TM), lambda i, j, *_: (i, 0, j))
    ha = pl.BlockSpec((1, N_HEADS, HA, TM), lambda i, j, *_: (i, 0, 0, j))
    return _host_call(
        "kvq_prompt", _kvq_prompt_pre, _kvq_prompt_main, (b, t // TM),
        in_specs=[
            pl.BlockSpec((1, TM, d), lambda i, j, *_: (i, j, 0)),
            pl.BlockSpec((1, d, 2), lambda i, j, *_: (i, 0, 0)),
            pl.BlockSpec((d, 4), lambda i, j, *_: (0, 0)),
            pl.BlockSpec((N_HEADS, 1), lambda i, j, *_: (0, 0)),
            pl.BlockSpec((TM, TM), lambda i, j, *_: (0, 0)),
            _resident((d, d)), _resident((d, d)), _resident((N_HEADS, d)), _resident((d, d)),
        ],
        args=[x, modc, pcol, bf_col, tri, wk_t, wv_t, wf_t, wq_t],
        out_specs=(ct, ct, pl.BlockSpec((1, N_HEADS, TM), lambda i, j, *_: (i, 0, j)), ha, ha, ha),
        out_shapes=out_shapes,
        scratch=[pltpu.VMEM((N_HEADS, 1), f32)],
        dec=dec)


def _attn_prompt_main(qa_ref, ka_ref, vb_ref, o_ref):
    hg, _, t = qa_ref.shape[1:]
    tq = TM
    key = lax.broadcasted_iota(jnp.int32, (tq, tq), 0)
    qry = lax.broadcasted_iota(jnp.int32, (tq, tq), 1)
    causal = key <= qry
    contract0 = (((0,), (0,)), ((), ()))

    def scores(h, i):
        q0 = i * tq
        qa = qa_ref[0, h, :, q0:q0 + tq]
        s_d = lax.dot_general(ka_ref[0, h, :, q0:q0 + tq], qa, contract0,
                              preferred_element_type=f32)
        s_d = jnp.where(causal, s_d, -jnp.inf)
        m = jnp.max(s_d, axis=0, keepdims=True)
        s_o = None
        if i > 0:
            s_o = lax.dot_general(ka_ref[0, h, :, 0:q0], qa, contract0,
                                  preferred_element_type=f32)
            m = jnp.maximum(m, jnp.max(s_o, axis=0, keepdims=True))
        return s_d, s_o, m

    def weighted_values(h, i, s_d, s_o, m):
        q0 = i * tq
        p_d = jnp.exp2(s_d - m)
        acc = jnp.dot(vb_ref[0, h, :, q0:q0 + tq], p_d.astype(bf16), preferred_element_type=f32)
        if i > 0:
            p_o = jnp.exp2(s_o - m)
            acc = acc + jnp.dot(vb_ref[0, h, :, 0:q0], p_o.astype(bf16), preferred_element_type=f32)
        o_ref[0, h, :, q0:q0 + tq] = (acc[0:HEAD_DIM] / acc[HEAD_DIM:HEAD_DIM + 1]).astype(bf16)

    units = [(h, i) for h in range(hg) for i in range(t // tq)]
    ahead = 3
    pending = [scores(*u) for u in units[:ahead]]
    for n, unit in enumerate(units):
        if n + ahead < len(units):
            pending.append(scores(*units[n + ahead]))
        weighted_values(*unit, *pending.pop(0))


def _attn_prompt(qa, ka, vb, dec):
    b, nh, _, t = qa.shape
    spec_a = pl.BlockSpec((1, ATT_HG, HA, t), lambda i, j, *_: (i, j, 0, 0))
    spec_o = pl.BlockSpec((1, ATT_HG, HEAD_DIM, t), lambda i, j, *_: (i, j, 0, 0))
    return _host_call(
        "attn_prompt", None, _attn_prompt_main, (b, nh // ATT_HG),
        in_specs=[spec_a, spec_a, spec_a], args=[qa, ka, vb],
        out_specs=(spec_o,), out_shapes=(jax.ShapeDtypeStruct((b, nh, HEAD_DIM, t), bf16),),
        scratch=[], dec=dec)


def _head_rms_rows(x, gmat_ref, w_row):
    ms = _dot3(x * x, gmat_ref[...])
    return x * lax.rsqrt(ms + EPS) * w_row


def _kvq_sample_body(x_ref, mod_ref, prow_ref, bf_ref, gmat_ref, wkvf_ref, wq_ref,
                     k_ref, v_ref, lf_ref, q_ref):
    d = D_MODEL
    xn = _rms_rows(x_ref[...])
    pr = prow_ref[...]
    hk = (xn * pr[0:1]).astype(bf16)
    hq = ((xn * pr[1:2]) * (1.0 + mod_ref[1]) + mod_ref[0]).astype(bf16)
    kvf = jnp.dot(hk, wkvf_ref[...], preferred_element_type=f32)
    qr = jnp.dot(hq, wq_ref[...], preferred_element_type=f32)
    k_ref[...] = _head_rms_rows(kvf[:, :d], gmat_ref, pr[2:3])
    v_ref[...] = kvf[:, d:2 * d]
    lf_ref[...] = _log_sigmoid(kvf[:, 2 * d:] + bf_ref[...])
    q_ref[...] = _head_rms_rows(qr, gmat_ref, pr[3:4])


def _kvq_sample(x, mod, prow, bf_row, gmat, w_kvf_pad, w_q):
    n, d = x.shape
    return pl.pallas_call(
        _kvq_sample_body,
        out_shape=(jax.ShapeDtypeStruct((n, d), f32), jax.ShapeDtypeStruct((n, d), f32),
                   jax.ShapeDtypeStruct((n, 128), f32), jax.ShapeDtypeStruct((n, d), f32)),
        compiler_params=_cparams(None, VMEM_LIMIT),
        name="kvq_sample",
    )(x, mod, prow, bf_row, gmat, w_kvf_pad, w_q)


def kernel(x_prompt, x_sample, state_conv_a, state_ffn, cache_k, cache_v, cache_logf, page_table,
           c_prompt, c_sample, norm_mix, norm_ffn, w_ada, b_ada, w_in_a, conv_w_a, w_out_a, kv_norm,
           w_kvf, b_f, k_norm, w_q, q_norm, w_o, w_up, conv_w_ffn, w_down):
    d = D_MODEL
    nb, t, _ = x_prompt.shape
    ns = x_sample.shape[0]
    assert sum(DEC_ROWS) == ns

    w_in = w_in_a[0].astype(bf16)
    w_out = w_out_a[0].astype(bf16)
    w_up_b = w_up.astype(bf16)
    w_down_b = w_down.astype(bf16)
    w_o_b = w_o[0].astype(bf16)
    w_q_b = w_q[0].astype(bf16)
    w_kvf_b = w_kvf.astype(bf16)
    wk_t = w_kvf_b[:, :d].T
    wv_t = w_kvf_b[:, d:2 * d].T
    wf_t = w_kvf_b[:, 2 * d:].T
    wq_t = w_q_b.T
    w_kvf_pad = jnp.pad(w_kvf_b, ((0, 0), (0, 128 - N_HEADS)))
    k_norm_d = jnp.tile(k_norm, N_HEADS)
    q_norm_d = jnp.tile(q_norm[0], N_HEADS)
    prow = jnp.stack([kv_norm, norm_mix[1], k_norm_d, q_norm_d])
    pcol = prow.T
    bf_row = jnp.pad(b_f, (0, 128 - N_HEADS))[None, :]
    bf_col = b_f[:, None]
    idx = jnp.arange(TM)
    tri = (idx[:, None] <= idx[None, :]).astype(bf16)
    pidx = jnp.arange(PAGE)
    rev = (pidx[:, None] > pidx[None, :]).astype(bf16)
    lane_head = jnp.arange(d) // HEAD_DIM
    gmat = ((lane_head[:, None] == lane_head[None, :]).astype(f32) / HEAD_DIM).astype(bf16)

    mod = _ada(jnp.concatenate([c_prompt, c_sample], axis=0), w_ada, b_ada)
    mod_p = mod[:, :nb].reshape(2, nb, 6, d)
    mod_s = mod[:, nb:].reshape(2, ns, 6, d).transpose(0, 2, 1, 3)
    modc_p = mod_p[1, :, 0:2].transpose(0, 2, 1)

    xs = x_sample[:, 0, :]
    xs1, u_s = _mix_sample(xs, mod_s[0], norm_mix[0:1], state_conv_a[0, :, 0], state_conv_a[0, :, 1],
                           w_in, conv_w_a[0], w_out)
    xs2, a0_s = _ffn_sample(xs1, mod_s[0], norm_ffn[0:1], state_ffn[0, :, 0], state_ffn[0, :, 1],
                            w_up_b[0], conv_w_ffn[0], w_down_b[0])
    k_s, v_s, lf_s, q_s = _kvq_sample(xs2, mod_s[1], prow, bf_row, gmat, w_kvf_pad, w_q_b)
    lf_s = lf_s[:, :N_HEADS]
    cache_kt = cache_k.transpose(0, 2, 3, 1)
    cache_vt = cache_v.transpose(0, 2, 3, 1)
    cache_lft = cache_logf.transpose(0, 2, 1)

    def dec_job(k):
        r0 = sum(DEC_ROWS[:k])
        rows = slice(r0, r0 + DEC_ROWS[k])
        return (page_table[rows], q_s[rows], k_s[rows], v_s[rows], lf_s[rows], rev,
                cache_kt, cache_vt, cache_lft)

    (x1, conv_a_p), o_s0 = _mix_prompt(x_prompt, mod_p[0], norm_mix[0:1], w_in, conv_w_a[0], w_out,
                                       dec_job(0))
    (x2, ffn0_p), o_s1 = _ffn_prompt(x1, mod_p[0], norm_ffn[0:1], w_up_b[0], conv_w_ffn[0], w_down_b[0],
                                     dec_job(1))
    (kt_p, vt_p, lft_p, qa, ka, vb), o_s2 = _kvq_prompt(x2, modc_p, pcol, bf_col, tri, wk_t, wv_t, wf_t,
                                                        wq_t, dec_job(2))
    (o_t,), o_s3 = _attn_prompt(qa, ka, vb, dec_job(3))
    (y_p, ffn1_p), o_s4 = _ffn_prompt(x2, mod_p[1], norm_ffn[1:2], w_up_b[1], conv_w_ffn[1], w_down_b[1],
                                      dec_job(4), o_t=o_t.reshape(nb, d, t), w_o=w_o_b)

    o_s = jnp.concatenate([o_s0, o_s1, o_s2, o_s3, o_s4], axis=0)
    y_s, a1_s = _ffn_sample(xs2, mod_s[1], norm_ffn[1:2], state_ffn[1, :, 0], state_ffn[1, :, 1],
                            w_up_b[1], conv_w_ffn[1], w_down_b[1], o=o_s, w_o=w_o_b)

    conv_a_s = jnp.stack([state_conv_a[0, :, 1], u_s], axis=1)[None]
    ffn_p = jnp.stack([ffn0_p, ffn1_p])
    ffn_s = jnp.stack([jnp.stack([state_ffn[0, :, 1], a0_s], axis=1),
                       jnp.stack([state_ffn[1, :, 1], a1_s], axis=1)])
    k_p = kt_p.reshape(nb, N_HEADS, HEAD_DIM, t).transpose(0, 3, 1, 2)
    v_p = vt_p.reshape(nb, N_HEADS, HEAD_DIM, t).transpose(0, 3, 1, 2)
    logf_p = lft_p.transpose(0, 2, 1)
    return (y_p, y_s[:, None, :], conv_a_p[None], conv_a_s, ffn_p, ffn_s,
            k_p, v_p, logf_p,
            k_s.reshape(ns, 1, N_HEADS, HEAD_DIM), v_s.reshape(ns, 1, N_HEADS, HEAD_DIM),
            lf_s[:, None, :])
```

```python
import functools

import jax
import jax.numpy as jnp
from jax import lax
from jax.experimental import pallas as pl
from jax.experimental.pallas import tpu as pltpu

f32 = jnp.float32
bf16 = jnp.bfloat16

D_MODEL = 1024
N_HEADS = 16
HEAD_DIM = D_MODEL // N_HEADS
D_FF = 2816
PAGE = 128
EPS = 1e-6
ATTN_SCALE = HEAD_DIM ** -0.5
LOG2E = 1.4426950408889634

TM = 256
TMK = 256
TMX = 512
TMF = 256
CHUNK = 256
AUG = 16
HA = HEAD_DIM + AUG
ATT_HG = 2
ADA_TN = 1024
VMEM_LIMIT = 56 * 1024 * 1024
DEC_ROWS = (4, 8, 4, 8, 8)


def _cparams(sem, vmem=None):
    return pltpu.CompilerParams(dimension_semantics=sem, vmem_limit_bytes=vmem)


def _resident(shape):
    nd = len(shape)
    return pl.BlockSpec(shape, lambda *_: (0,) * nd, pipeline_mode=pl.Buffered(1))


def _rms_rows(x):
    return x * lax.rsqrt(jnp.mean(x * x, axis=-1, keepdims=True) + EPS)


def _modulate_rows(x, g, shift, scale):
    return (_rms_rows(x) * g) * (1.0 + scale) + shift


def _gelu_tanh(x):
    c = 0.7978845608028654
    return x * (0.5 * (1.0 + jnp.tanh(c * (x + 0.044715 * (x * x * x)))))


def _log_sigmoid(x):
    return jnp.minimum(x, 0.0) - jnp.log1p(jnp.exp(-jnp.abs(x)))


def _split3(x):
    hi = x.astype(bf16)
    r1 = x - hi.astype(f32)
    mid = r1.astype(bf16)
    lo = (r1 - mid.astype(f32)).astype(bf16)
    return hi, mid, lo


def _dot3(x, m_bf16):
    hi, mid, lo = _split3(x)
    d = functools.partial(jnp.dot, preferred_element_type=f32)
    return (d(hi, m_bf16) + d(mid, m_bf16)) + d(lo, m_bf16)


def _ada_body(c_ref, w_ref, b_ref, o_ref):
    c = c_ref[...]
    s = c * (1.0 / (1.0 + jnp.exp(-c)))
    o_ref[0] = jnp.dot(s.astype(bf16), w_ref[0].astype(bf16), preferred_element_type=f32) + b_ref[0]


def _ada(c_all, w_ada, b_ada):
    n_layers, _, n_out = w_ada.shape
    rows = c_all.shape[0]
    return pl.pallas_call(
        _ada_body,
        out_shape=jax.ShapeDtypeStruct((n_layers, rows, n_out), f32),
        grid=(n_layers, n_out // ADA_TN),
        in_specs=[
            pl.BlockSpec((rows, D_MODEL), lambda l, j: (0, 0)),
            pl.BlockSpec((1, D_MODEL, ADA_TN), lambda l, j: (l, 0, j)),
            pl.BlockSpec((1, 1, ADA_TN), lambda l, j: (l, 0, j)),
        ],
        out_specs=pl.BlockSpec((1, rows, ADA_TN), lambda l, j: (l, 0, j)),
        compiler_params=_cparams(("parallel", "parallel")),
        name="ada",
    )(c_all, w_ada, b_ada.reshape(n_layers, 1, n_out))


def _own_lanes():
    shape = (N_HEADS, D_MODEL)
    return lax.broadcasted_iota(jnp.int32, shape, 1) // HEAD_DIM == lax.broadcasted_iota(jnp.int32, shape, 0)


def _block_diag(row):
    return jnp.where(_own_lanes(), jnp.broadcast_to(row, (N_HEADS, D_MODEL)), 0.0)


def _decode_init(o_ref, scratch):
    m_ref, l_ref, suf_ref, acc_ref = scratch
    m_ref[...] = jnp.full(m_ref.shape, -jnp.inf, f32)
    l_ref[...] = jnp.zeros(l_ref.shape, f32)
    suf_ref[...] = jnp.zeros(suf_ref.shape, f32)
    acc_ref[...] = jnp.zeros(acc_ref.shape, f32)
    o_ref[...] = jnp.zeros(o_ref.shape, f32)


def _decode_scores(q_ref, fn_ref, rev_ref, k_page, lf_page, scratch, g):
    m_ref, l_ref, suf_ref, _ = scratch
    fn = fn_ref[0]
    qbd_b = _block_diag(q_ref[0]).astype(bf16)
    carry = suf_ref[...]
    lf_all = jnp.concatenate([lf_page(j) for j in range(g)], axis=0)
    suf_all = _dot3(lf_all, rev_ref[...])
    tot_all = jnp.sum(lf_all, axis=1, keepdims=True)
    scores = []
    for j0 in range(0, g, 2):
        kb = jnp.concatenate([k_page(j).reshape(D_MODEL, PAGE).astype(bf16) for j in (j0, j0 + 1)],
                             axis=1)
        s2 = jnp.dot(qbd_b, kb, preferred_element_type=f32) * ATTN_SCALE
        for j in (j0, j0 + 1):
            rows = slice(j * N_HEADS, (j + 1) * N_HEADS)
            suf = suf_all[rows] + carry
            carry = carry + tot_all[rows]
            scores.append(s2[:, (j - j0) * PAGE:(j - j0 + 1) * PAGE] + fn + suf)
    suf_ref[...] = carry

    m_old = m_ref[...]
    m_new = m_old
    for s in scores:
        m_new = jnp.maximum(m_new, jnp.max(s, axis=1, keepdims=True))
    alpha = jnp.exp(m_old - m_new)
    probs = [jnp.exp(s - m_new) for s in scores]
    l_new = l_ref[...] * alpha
    for p in probs:
        l_new = l_new + jnp.sum(p, axis=1, keepdims=True)
    m_ref[...] = m_new
    l_ref[...] = l_new
    return probs, alpha


def _decode_values(v_page, scratch, g, probs, alpha):
    acc_ref = scratch[3]
    alpha_b = jnp.broadcast_to(alpha, (N_HEADS, PAGE))
    for h in range(N_HEADS):
        a = acc_ref[h] * alpha_b[h:h + 1, :]
        for j in range(g):
            a = a + v_page(j, h) * probs[j][h:h + 1, :]
        acc_ref[h] = a


def _decode_finish(ins, o_ref, scratch):
    q_ref, kn_ref, vn_ref, fn_ref = ins[:4]
    m_ref, l_ref, _, acc_ref = scratch
    fn = fn_ref[0]
    own = _own_lanes()
    shape = (N_HEADS, D_MODEL)

    def spread(c):
        return jnp.sum(jnp.where(own, jnp.broadcast_to(c, shape), 0.0), axis=0, keepdims=True)

    qbd = _block_diag(q_ref[0])
    s_new = jnp.sum(qbd * jnp.broadcast_to(kn_ref[0], shape), axis=1, keepdims=True) * ATTN_SCALE
    s_new = (s_new + fn) - fn
    m_old = m_ref[...]
    m_fin = jnp.maximum(m_old, s_new)
    a_fin = jnp.exp(m_old - m_fin)
    p_new = jnp.exp(s_new - m_fin)
    l_fin = l_ref[...] * a_fin + p_new
    o_col = jnp.sum(acc_ref[...], axis=2, keepdims=True).reshape(D_MODEL, 1)
    o_row = jnp.broadcast_to(o_col, (D_MODEL, 128)).T[0:1, :]
    o_ref[0] = (o_row * spread(a_fin) + vn_ref[0] * spread(p_new)) / spread(l_fin)


def _host_call(name, pre, main, grid, in_specs, args, out_specs, out_shapes, scratch, dec):
    pt, q, kn, vn, fn, rev, ckt, cvt, clft = dec
    rows, n_pages = pt.shape
    nj = grid[1]
    steps = grid[0] * nj
    assert steps % rows == 0 and n_pages % (steps // rows) == 0
    spr = steps // rows
    g = n_pages // spr
    n_in, n_out, n_scr, n_dec = len(args), len(out_shapes), len(scratch), 8

    def row_of(i, j):
        return (i * nj + j) // spr

    row_spec = pl.BlockSpec((1, 1, D_MODEL), lambda i, j, ptr: (row_of(i, j), 0, 0))
    hbm_spec = pl.BlockSpec(memory_space=pl.ANY)
    dec_specs = [row_spec, row_spec, row_spec,
                 pl.BlockSpec((1, N_HEADS, 1), lambda i, j, ptr: (row_of(i, j), 0, 0)),
                 pl.BlockSpec((PAGE, PAGE), lambda i, j, ptr: (0, 0)),
                 hbm_spec, hbm_spec, hbm_spec]
    dec_args = [q[:, None, :], kn[:, None, :], vn[:, None, :], fn[:, :, None], rev, ckt, cvt, clft]
    dec_scratch = [
        pltpu.VMEM((N_HEADS, 1), f32),
        pltpu.VMEM((N_HEADS, 1), f32),
        pltpu.VMEM((N_HEADS, 1), f32),
        pltpu.VMEM((N_HEADS, HEAD_DIM, PAGE), f32),
        pltpu.VMEM((2 * g, N_HEADS, HEAD_DIM, PAGE), f32),
        pltpu.VMEM((2 * g, N_HEADS, HEAD_DIM, PAGE), f32),
        pltpu.VMEM((2 * g, N_HEADS, PAGE), f32),
        pltpu.SemaphoreType.DMA((2, 3, g)),
    ]

    def body(pt_ref, *refs):
        host_in, dec_in = refs[:n_in], refs[n_in:n_in + n_dec]
        rest = refs[n_in + n_dec:]
        host_out, o_dec = rest[:n_out], rest[n_out]
        host_scr, dec_rest = rest[n_out + 1:n_out + 1 + n_scr], rest[n_out + 1 + n_scr:]
        dec_scr, (kbuf, vbuf, lfbuf, sem) = dec_rest[:4], dec_rest[4:]
        q_ref, _, _, fn_ref, rev_ref, ckt_ref, cvt_ref, clft_ref = dec_in
        host_refs = (*host_in, *host_out, *host_scr)
        lin = pl.program_id(0) * nj + pl.program_id(1)
        pstep = lin % spr
        slot = lin % 2

        def page_copies(step, to_slot):
            row = step // spr
            first_page = n_pages - 1 - (step % spr) * g
            copies = []
            for jj in range(g):
                phys = pt_ref[row, first_page - jj]
                dst = to_slot * g + jj
                copies.append(pltpu.make_async_copy(ckt_ref.at[phys], kbuf.at[dst], sem.at[to_slot, 0, jj]))
                copies.append(pltpu.make_async_copy(cvt_ref.at[phys], vbuf.at[dst], sem.at[to_slot, 1, jj]))
                copies.append(pltpu.make_async_copy(clft_ref.at[phys], lfbuf.at[dst], sem.at[to_slot, 2, jj]))
            return copies

        @pl.when(lin == 0)
        def _():
            for c in page_copies(0, 0):
                c.start()

        @pl.when(lin + 1 < steps)
        def _():
            for c in page_copies(lin + 1, 1 - slot):
                c.start()

        if pre is not None:
            pre(*host_refs)

        @pl.when(pstep == 0)
        def _():
            _decode_init(o_dec, dec_scr)

        for c in page_copies(lin, slot):
            c.wait()

        probs, alpha = _decode_scores(q_ref, fn_ref, rev_ref, lambda j: kbuf[slot * g + j],
                                      lambda j: lfbuf[slot * g + j], dec_scr, g)
        main(*host_refs)

        p_max = functools.reduce(jnp.maximum, probs)

        @pl.when(jnp.max(p_max) > 0.0)
        def _():
            _decode_values(lambda j, h: vbuf[slot * g + j, h], dec_scr, g, probs, alpha)

        @pl.when(pstep == spr - 1)
        def _():
            _decode_finish(dec_in, o_dec, dec_scr)

    grid_spec = pltpu.PrefetchScalarGridSpec(
        num_scalar_prefetch=1,
        grid=grid,
        in_specs=list(in_specs) + dec_specs,
        out_specs=(*out_specs, row_spec),
        scratch_shapes=list(scratch) + dec_scratch,
    )
    outs = pl.pallas_call(
        body,
        out_shape=(*out_shapes, jax.ShapeDtypeStruct((rows, 1, D_MODEL), f32)),
        grid_spec=grid_spec,
        compiler_params=_cparams(("arbitrary", "arbitrary"), VMEM_LIMIT),
        name=name,
    )(pt, *args, *dec_args)
    return outs[:-1], outs[-1][:, 0, :]


def _conv_start(buf):
    @pl.when(pl.program_id(1) == 0)
    def _():
        buf[0:8, :] = jnp.zeros((8, buf.shape[1]), f32)


def _conv_chunk(u, cw, buf, c0):
    tm, w = u.shape
    cols = slice(c0, c0 + w)
    buf[8:tm + 8, cols] = u
    return (buf[6:tm + 6, cols] * cw[0:1, cols] + buf[7:tm + 7, cols] * cw[1:2, cols]
            + u * cw[2:3, cols])


def _conv_finish(buf, st_ref, tm):
    st_ref[0] = buf[tm + 6:tm + 8, :]
    buf[0:8, :] = buf[tm:tm + 8, :]


def _chunk_pipeline(n, produce, consume):
    ahead = 2
    pending = [produce(c) for c in range(min(ahead, n))]
    total = None
    for c in range(n):
        if c + ahead < n:
            pending.append(produce(c + ahead))
        part = consume(c, *pending.pop(0))
        total = part if total is None else total + part
    return total


def _mix_prompt_pre(x_ref, mod_ref, g_ref, win_ref, cw_ref, wout_ref, o_ref, st_ref, ubuf):
    _conv_start(ubuf)


def _mix_prompt_main(x_ref, mod_ref, g_ref, win_ref, cw_ref, wout_ref, o_ref, st_ref, ubuf):
    d = D_MODEL
    tm = x_ref.shape[1]
    x = x_ref[0]
    m = mod_ref[0]
    h = _modulate_rows(x, g_ref[...], m[0:1], m[1:2]).astype(bf16)
    z = jnp.dot(h, win_ref[...], preferred_element_type=f32)
    uc = _conv_chunk(z[:, d:2 * d] * z[:, 2 * d:], cw_ref[...], ubuf, 0)
    _conv_finish(ubuf, st_ref, tm)
    y = jnp.dot((z[:, :d] * uc).astype(bf16), wout_ref[...], preferred_element_type=f32)
    o_ref[0] = x + m[2:3] * y


def _mix_prompt(x, mod, g, w_in, cw, w_out, dec):
    b, t, d = x.shape
    xspec = pl.BlockSpec((1, TMX, d), lambda i, j, *_: (i, j, 0))
    return _host_call(
        "mix_prompt", _mix_prompt_pre, _mix_prompt_main, (b, t // TMX),
        in_specs=[
            xspec,
            pl.BlockSpec((1, 6, d), lambda i, j, *_: (i, 0, 0)),
            pl.BlockSpec((1, d), lambda i, j, *_: (0, 0)),
            _resident((d, 3 * d)),
            pl.BlockSpec((3, d), lambda i, j, *_: (0, 0)),
            _resident((d, d)),
        ],
        args=[x, mod, g, w_in, cw, w_out],
        out_specs=(xspec, pl.BlockSpec((1, 2, d), lambda i, j, *_: (i, 0, 0))),
        out_shapes=(jax.ShapeDtypeStruct((b, t, d), f32), jax.ShapeDtypeStruct((b, 2, d), f32)),
        scratch=[pltpu.VMEM((TMX + 8, d), f32)],
        dec=dec)


def _mix_sample_body(x_ref, mod_ref, g_ref, p0_ref, p1_ref, win_ref, cw_ref, wout_ref, o_ref, u_ref):
    d = D_MODEL
    x = x_ref[...]
    h = _modulate_rows(x, g_ref[...], mod_ref[0], mod_ref[1])
    z = jnp.dot(h.astype(bf16), win_ref[...], preferred_element_type=f32)
    u = z[:, d:2 * d] * z[:, 2 * d:]
    cw = cw_ref[...]
    uc = p0_ref[...] * cw[0:1] + p1_ref[...] * cw[1:2] + u * cw[2:3]
    u_ref[...] = u
    y = jnp.dot((z[:, :d] * uc).astype(bf16), wout_ref[...], preferred_element_type=f32)
    o_ref[...] = x + mod_ref[2] * y


def _mix_sample(x, mod, g, past0, past1, w_in, cw, w_out):
    n, d = x.shape
    return pl.pallas_call(
        _mix_sample_body,
        out_shape=(jax.ShapeDtypeStruct((n, d), f32), jax.ShapeDtypeStruct((n, d), f32)),
        compiler_params=_cparams(None, VMEM_LIMIT),
        name="mix_sample",
    )(x, mod, g, past0, past1, w_in, cw, w_out)


def _ffn_prompt_pre(with_attn, *refs):
    _conv_start(refs[-1])


def _ffn_prompt_main(with_attn, *refs):
    if with_attn:
        (x_ref, ot_ref, wo_ref, mod_ref, g_ref, wup_ref, cw_ref, wdown_ref, o_ref, st_ref, abuf) = refs
    else:
        (x_ref, mod_ref, g_ref, wup_ref, cw_ref, wdown_ref, o_ref, st_ref, abuf) = refs
    tm = x_ref.shape[1]
    x = x_ref[0]
    m = mod_ref[0]
    if with_attn:
        attn = lax.dot_general(ot_ref[0], wo_ref[...], (((0,), (0,)), ((), ())),
                               preferred_element_type=f32)
        x = x + m[2:3] * attn
    h = _modulate_rows(x, g_ref[...], m[3:4], m[4:5]).astype(bf16)
    cw = cw_ref[...]
    dot = functools.partial(jnp.dot, preferred_element_type=f32)

    def up(c):
        c0 = c * CHUNK
        return dot(h, wup_ref[:, c0:c0 + CHUNK]), dot(h, wup_ref[:, D_FF + c0:D_FF + c0 + CHUNK])

    def down(c, a, v):
        c0 = c * CHUNK
        gl = _gelu_tanh(_conv_chunk(a, cw, abuf, c0)) * v
        return dot(gl.astype(bf16), wdown_ref[c0:c0 + CHUNK, :])

    y = _chunk_pipeline(D_FF // CHUNK, up, down)
    _conv_finish(abuf, st_ref, tm)
    o_ref[0] = x + m[5:6] * y


def _ffn_prompt(x, mod, g, w_up, cw, w_down, dec, o_t=None, w_o=None):
    b, t, d = x.shape
    with_attn = o_t is not None
    xspec = pl.BlockSpec((1, TMF, d), lambda i, j, *_: (i, j, 0))
    in_specs = [xspec]
    args = [x]
    if with_attn:
        in_specs += [pl.BlockSpec((1, d, TMF), lambda i, j, *_: (i, 0, j)), _resident((d, d))]
        args += [o_t, w_o]
    in_specs += [
        pl.BlockSpec((1, 6, d), lambda i, j, *_: (i, 0, 0)),
        pl.BlockSpec((1, d), lambda i, j, *_: (0, 0)),
        _resident((d, 2 * D_FF)),
        pl.BlockSpec((3, D_FF), lambda i, j, *_: (0, 0)),
        _resident((D_FF, d)),
    ]
    args += [mod, g, w_up, cw, w_down]
    return _host_call(
        "ffn_attn_prompt" if with_attn else "ffn_prompt",
        functools.partial(_ffn_prompt_pre, with_attn), functools.partial(_ffn_prompt_main, with_attn),
        (b, t // TMF), in_specs=in_specs, args=args,
        out_specs=(xspec, pl.BlockSpec((1, 2, D_FF), lambda i, j, *_: (i, 0, 0))),
        out_shapes=(jax.ShapeDtypeStruct((b, t, d), f32), jax.ShapeDtypeStruct((b, 2, D_FF), f32)),
        scratch=[pltpu.VMEM((TMF + 8, D_FF), f32)],
        dec=dec)


def _ffn_sample_body(with_attn, *refs):
    if with_attn:
        (x_ref, o_ref_in, wo_ref, mod_ref, g_ref, p0_ref, p1_ref, wup_ref, cw_ref, wdown_ref,
         y_ref, a_ref) = refs
    else:
        (x_ref, mod_ref, g_ref, p0_ref, p1_ref, wup_ref, cw_ref, wdown_ref, y_ref, a_ref) = refs
    x = x_ref[...]
    if with_attn:
        attn = jnp.dot(o_ref_in[...].astype(bf16), wo_ref[...], preferred_element_type=f32)
        x = x + mod_ref[2] * attn
    cw = cw_ref[...]
    h = _modulate_rows(x, g_ref[...], mod_ref[3], mod_ref[4])
    up = jnp.dot(h.astype(bf16), wup_ref[...], preferred_element_type=f32)
    a = up[:, :D_FF]
    a_ref[...] = a
    ac = p0_ref[...] * cw[0:1] + p1_ref[...] * cw[1:2] + a * cw[2:3]
    gl = _gelu_tanh(ac) * up[:, D_FF:]
    y = jnp.dot(gl.astype(bf16), wdown_ref[...], preferred_element_type=f32)
    y_ref[...] = x + mod_ref[5] * y


def _ffn_sample(x, mod, g, past0, past1, w_up, cw, w_down, o=None, w_o=None):
    n, d = x.shape
    with_attn = o is not None
    args = [x] + ([o, w_o] if with_attn else []) + [mod, g, past0, past1, w_up, cw, w_down]
    return pl.pallas_call(
        functools.partial(_ffn_sample_body, with_attn),
        out_shape=(jax.ShapeDtypeStruct((n, d), f32), jax.ShapeDtypeStruct((n, D_FF), f32)),
        compiler_params=_cparams(None, VMEM_LIMIT),
        name="ffn_attn_sample" if with_attn else "ffn_sample",
    )(*args)


def _aug_rows(f_row, negate_first):
    hi, mid, lo = (p.astype(f32) for p in _split3(f_row))
    tm = f_row.shape[1]
    row = lax.broadcasted_iota(jnp.int32, (AUG, tm), 0)
    one = jnp.ones((AUG, tm), f32)
    zero = jnp.zeros((AUG, tm), f32)
    if negate_first:
        parts = (-hi, -mid, -lo, one, one, one)
    else:
        parts = (one, one, one, hi, mid, lo)
    out = zero
    for r, p in enumerate(parts):
        out = jnp.where(row == r, jnp.broadcast_to(p, (AUG, tm)), out)
    return out


def _kvq_prompt_pre(*refs):
    carry = refs[-1]

    @pl.when(pl.program_id(1) == 0)
    def _():
        carry[...] = jnp.zeros(carry.shape, f32)


def _kvq_prompt_main(x_ref, modc_ref, pcol_ref, bf_ref, tri_ref, wk_ref, wv_ref, wf_ref, wq_ref,
                     kt_ref, vt_ref, lf_ref, qa_ref, ka_ref, vb_ref, carry):
    tm = x_ref.shape[1]
    xt = x_ref[0].T
    xn = xt * lax.rsqrt(jnp.mean(xt * xt, axis=0, keepdims=True) + EPS)
    pc = pcol_ref[...]
    mc = modc_ref[0]
    hk = (xn * pc[:, 0:1]).astype(bf16)
    hq = ((xn * pc[:, 1:2]) * (1.0 + mc[:, 1:2]) + mc[:, 0:1]).astype(bf16)
    kt = jnp.dot(wk_ref[...], hk, preferred_element_type=f32)
    vt = jnp.dot(wv_ref[...], hk, preferred_element_type=f32)
    ft = jnp.dot(wf_ref[...], hk, preferred_element_type=f32)
    qt = jnp.dot(wq_ref[...], hq, preferred_element_type=f32)

    logf = _log_sigmoid(ft + bf_ref[...])
    lf_ref[0] = logf
    cum = _dot3(logf, tri_ref[...]) + carry[...]
    carry[...] = cum[:, tm - 1:tm]

    vt_ref[0] = vt
    vb_ref[0, :, 0:HEAD_DIM, :] = vt.astype(bf16).reshape(N_HEADS, HEAD_DIM, tm)
    ones_row = lax.broadcasted_iota(jnp.int32, (N_HEADS, AUG, tm), 1) == 0
    vb_ref[0, :, HEAD_DIM:HA, :] = ones_row.astype(bf16)
    for h in range(N_HEADS):
        r0, r1 = h * HEAD_DIM, (h + 1) * HEAD_DIM
        kh = kt[r0:r1]
        kh = kh * lax.rsqrt(jnp.mean(kh * kh, axis=0, keepdims=True) + EPS) * pc[r0:r1, 2:3]
        kt_ref[0, r0:r1, :] = kh
        qh = qt[r0:r1]
        qh = qh * lax.rsqrt(jnp.mean(qh * qh, axis=0, keepdims=True) + EPS) * pc[r0:r1, 3:4]
        f_row = cum[h:h + 1] * LOG2E
        ka_ref[0, h, 0:HEAD_DIM, :] = kh.astype(bf16)
        ka_ref[0, h, HEAD_DIM:HA, :] = _aug_rows(f_row, True).astype(bf16)
        qa_ref[0, h, 0:HEAD_DIM, :] = (qh * (ATTN_SCALE * LOG2E)).astype(bf16)
        qa_ref[0, h, HEAD_DIM:HA, :] = _aug_rows(f_row, False).astype(bf16)


def _kvq_prompt(x, modc, pcol, bf_col, tri, wk_t, wv_t, wf_t, wq_t, dec):
    b, t, d = x.shape
    out_shapes = (
        jax.ShapeDtypeStruct((b, d, t), f32),
        jax.ShapeDtypeStruct((b, d, t), f32),
        jax.ShapeDtypeStruct((b, N_HEADS, t), f32),
        jax.ShapeDtypeStruct((b, N_HEADS, HA, t), bf16),
        jax.ShapeDtypeStruct((b, N_HEADS, HA, t), bf16),
        jax.ShapeDtypeStruct((b, N_HEADS, HA, t), bf16),
    )
    ct = pl.BlockSpec((1, d, TMK), lambda i, j, *_: (i, 0, j))
    ha = pl.BlockSpec((1, N_HEADS, HA, TMK), lambda i, j, *_: (i, 0, 0, j))
    return _host_call(
        "kvq_prompt", _kvq_prompt_pre, _kvq_prompt_main, (b, t // TMK),
        in_specs=[
            pl.BlockSpec((1, TMK, d), lambda i, j, *_: (i, j, 0)),
            pl.BlockSpec((1, d, 2), lambda i, j, *_: (i, 0, 0)),
            pl.BlockSpec((d, 4), lambda i, j, *_: (0, 0)),
            pl.BlockSpec((N_HEADS, 1), lambda i, j, *_: (0, 0)),
            pl.BlockSpec((TMK, TMK), lambda i, j, *_: (0, 0)),
            _resident((d, d)), _resident((d, d)), _resident((N_HEADS, d)), _resident((d, d)),
        ],
        args=[x, modc, pcol, bf_col, tri, wk_t, wv_t, wf_t, wq_t],
        out_specs=(ct, ct, pl.BlockSpec((1, N_HEADS, TMK), lambda i, j, *_: (i, 0, j)), ha, ha, ha),
        out_shapes=out_shapes,
        scratch=[pltpu.VMEM((N_HEADS, 1), f32)],
        dec=dec)


def _attn_prompt_main(qa_ref, ka_ref, vb_ref, o_ref):
    hg, _, t = qa_ref.shape[1:]
    tq = TM
    key = lax.broadcasted_iota(jnp.int32, (tq, tq), 0)
    qry = lax.broadcasted_iota(jnp.int32, (tq, tq), 1)
    causal = key <= qry
    contract0 = (((0,), (0,)), ((), ()))

    def scores(h, i):
        q0 = i * tq
        qa = qa_ref[0, h, :, q0:q0 + tq]
        s_d = lax.dot_general(ka_ref[0, h, :, q0:q0 + tq], qa, contract0,
                              preferred_element_type=f32)
        s_d = jnp.where(causal, s_d, -jnp.inf)
        m = jnp.max(s_d, axis=0, keepdims=True)
        s_o = None
        if i > 0:
            s_o = lax.dot_general(ka_ref[0, h, :, 0:q0], qa, contract0,
                                  preferred_element_type=f32)
            m = jnp.maximum(m, jnp.max(s_o, axis=0, keepdims=True))
        return s_d, s_o, m

    def weighted_values(h, i, s_d, s_o, m):
        q0 = i * tq
        p_d = jnp.exp2(s_d - m)
        acc = jnp.dot(vb_ref[0, h, :, q0:q0 + tq], p_d.astype(bf16), preferred_element_type=f32)
        if i > 0:
            p_o = jnp.exp2(s_o - m)
            acc = acc + jnp.dot(vb_ref[0, h, :, 0:q0], p_o.astype(bf16), preferred_element_type=f32)
        o_ref[0, h, :, q0:q0 + tq] = (acc[0:HEAD_DIM] / acc[HEAD_DIM:HEAD_DIM + 1]).astype(bf16)

    nq = t // tq
    units = [(h, i if h % 2 == 0 else nq - 1 - i) for h in range(hg) for i in range(nq)]
    ahead = 3
    pending = [scores(*u) for u in units[:ahead]]
    for n, unit in enumerate(units):
        if n + ahead < len(units):
            pending.append(scores(*units[n + ahead]))
        weighted_values(*unit, *pending.pop(0))


def _attn_prompt(qa, ka, vb, dec):
    b, nh, _, t = qa.shape
    spec_a = pl.BlockSpec((1, ATT_HG, HA, t), lambda i, j, *_: (i, j, 0, 0))
    spec_o = pl.BlockSpec((1, ATT_HG, HEAD_DIM, t), lambda i, j, *_: (i, j, 0, 0))
    return _host_call(
        "attn_prompt", None, _attn_prompt_main, (b, nh // ATT_HG),
        in_specs=[spec_a, spec_a, spec_a], args=[qa, ka, vb],
        out_specs=(spec_o,), out_shapes=(jax.ShapeDtypeStruct((b, nh, HEAD_DIM, t), bf16),),
        scratch=[], dec=dec)


def _head_rms_rows(x, gmat_ref, w_row):
    ms = _dot3(x * x, gmat_ref[...])
    return x * lax.rsqrt(ms + EPS) * w_row


def _kvq_sample_body(x_ref, mod_ref, prow_ref, bf_ref, gmat_ref, wkvf_ref, wq_ref,
                     k_ref, v_ref, lf_ref, q_ref):
    d = D_MODEL
    xn = _rms_rows(x_ref[...])
    pr = prow_ref[...]
    hk = (xn * pr[0:1]).astype(bf16)
    hq = ((xn * pr[1:2]) * (1.0 + mod_ref[1]) + mod_ref[0]).astype(bf16)
    kvf = jnp.dot(hk, wkvf_ref[...], preferred_element_type=f32)
    qr = jnp.dot(hq, wq_ref[...], preferred_element_type=f32)
    k_ref[...] = _head_rms_rows(kvf[:, :d], gmat_ref, pr[2:3])
    v_ref[...] = kvf[:, d:2 * d]
    lf_ref[...] = _log_sigmoid(kvf[:, 2 * d:] + bf_ref[...])
    q_ref[...] = _head_rms_rows(qr, gmat_ref, pr[3:4])


def _kvq_sample(x, mod, prow, bf_row, gmat, w_kvf_pad, w_q):
    n, d = x.shape
    return pl.pallas_call(
        _kvq_sample_body,
        out_shape=(jax.ShapeDtypeStruct((n, d), f32), jax.ShapeDtypeStruct((n, d), f32),
                   jax.ShapeDtypeStruct((n, 128), f32), jax.ShapeDtypeStruct((n, d), f32)),
        compiler_params=_cparams(None, VMEM_LIMIT),
        name="kvq_sample",
    )(x, mod, prow, bf_row, gmat, w_kvf_pad, w_q)


def kernel(x_prompt, x_sample, state_conv_a, state_ffn, cache_k, cache_v, cache_logf, page_table,
           c_prompt, c_sample, norm_mix, norm_ffn, w_ada, b_ada, w_in_a, conv_w_a, w_out_a, kv_norm,
           w_kvf, b_f, k_norm, w_q, q_norm, w_o, w_up, conv_w_ffn, w_down):
    d = D_MODEL
    nb, t, _ = x_prompt.shape
    ns = x_sample.shape[0]
    assert sum(DEC_ROWS) == ns

    w_in = w_in_a[0].astype(bf16)
    w_out = w_out_a[0].astype(bf16)
    w_up_b = w_up.astype(bf16)
    w_down_b = w_down.astype(bf16)
    w_o_b = w_o[0].astype(bf16)
    w_q_b = w_q[0].astype(bf16)
    w_kvf_b = w_kvf.astype(bf16)
    wk_t = w_kvf_b[:, :d].T
    wv_t = w_kvf_b[:, d:2 * d].T
    wf_t = w_kvf_b[:, 2 * d:].T
    wq_t = w_q_b.T
    w_kvf_pad = jnp.pad(w_kvf_b, ((0, 0), (0, 128 - N_HEADS)))
    k_norm_d = jnp.tile(k_norm, N_HEADS)
    q_norm_d = jnp.tile(q_norm[0], N_HEADS)
    prow = jnp.stack([kv_norm, norm_mix[1], k_norm_d, q_norm_d])
    pcol = prow.T
    bf_row = jnp.pad(b_f, (0, 128 - N_HEADS))[None, :]
    bf_col = b_f[:, None]
    idx = jnp.arange(TMK)
    tri = (idx[:, None] <= idx[None, :]).astype(bf16)
    pidx = jnp.arange(PAGE)
    rev = (pidx[:, None] > pidx[None, :]).astype(bf16)
    lane_head = jnp.arange(d) // HEAD_DIM
    gmat = ((lane_head[:, None] == lane_head[None, :]).astype(f32) / HEAD_DIM).astype(bf16)

    mod = _ada(jnp.concatenate([c_prompt, c_sample], axis=0), w_ada, b_ada)
    mod_p = mod[:, :nb].reshape(2, nb, 6, d)
    mod_s = mod[:, nb:].reshape(2, ns, 6, d).transpose(0, 2, 1, 3)
    modc_p = mod_p[1, :, 0:2].transpose(0, 2, 1)

    xs = x_sample[:, 0, :]
    xs1, u_s = _mix_sample(xs, mod_s[0], norm_mix[0:1], state_conv_a[0, :, 0], state_conv_a[0, :, 1],
                           w_in, conv_w_a[0], w_out)
    xs2, a0_s = _ffn_sample(xs1, mod_s[0], norm_ffn[0:1], state_ffn[0, :, 0], state_ffn[0, :, 1],
                            w_up_b[0], conv_w_ffn[0], w_down_b[0])
    k_s, v_s, lf_s, q_s = _kvq_sample(xs2, mod_s[1], prow, bf_row, gmat, w_kvf_pad, w_q_b)
    lf_s = lf_s[:, :N_HEADS]
    cache_kt = cache_k.transpose(0, 2, 3, 1)
    cache_vt = cache_v.transpose(0, 2, 3, 1)
    cache_lft = cache_logf.transpose(0, 2, 1)

    def dec_job(k):
        r0 = sum(DEC_ROWS[:k])
        rows = slice(r0, r0 + DEC_ROWS[k])
        return (page_table[rows], q_s[rows], k_s[rows], v_s[rows], lf_s[rows], rev,
                cache_kt, cache_vt, cache_lft)

    (x1, conv_a_p), o_s0 = _mix_prompt(x_prompt, mod_p[0], norm_mix[0:1], w_in, conv_w_a[0], w_out,
                                       dec_job(0))
    (x2, ffn0_p), o_s1 = _ffn_prompt(x1, mod_p[0], norm_ffn[0:1], w_up_b[0], conv_w_ffn[0], w_down_b[0],
                                     dec_job(1))
    (kt_p, vt_p, lft_p, qa, ka, vb), o_s2 = _kvq_prompt(x2, modc_p, pcol, bf_col, tri, wk_t, wv_t, wf_t,
                                                        wq_t, dec_job(2))
    (o_t,), o_s3 = _attn_prompt(qa, ka, vb, dec_job(3))
    (y_p, ffn1_p), o_s4 = _ffn_prompt(x2, mod_p[1], norm_ffn[1:2], w_up_b[1], conv_w_ffn[1], w_down_b[1],
                                      dec_job(4), o_t=o_t.reshape(nb, d, t), w_o=w_o_b)

    o_s = jnp.concatenate([o_s0, o_s1, o_s2, o_s3, o_s4], axis=0)
    y_s, a1_s = _ffn_sample(xs2, mod_s[1], norm_ffn[1:2], state_ffn[1, :, 0], state_ffn[1, :, 1],
                            w_up_b[1], conv_w_ffn[1], w_down_b[1], o=o_s, w_o=w_o_b)

    conv_a_s = jnp.stack([state_conv_a[0, :, 1], u_s], axis=1)[None]
    ffn_p = jnp.stack([ffn0_p, ffn1_p])
    ffn_s = jnp.stack([jnp.stack([state_ffn[0, :, 1], a0_s], axis=1),
                       jnp.stack([state_ffn[1, :, 1], a1_s], axis=1)])
    k_p = kt_p.reshape(nb, N_HEADS, HEAD_DIM, t).transpose(0, 3, 1, 2)
    v_p = vt_p.reshape(nb, N_HEADS, HEAD_DIM, t).transpose(0, 3, 1, 2)
    logf_p = lft_p.transpose(0, 2, 1)
    return (y_p, y_s[:, None, :], conv_a_p[None], conv_a_s, ffn_p, ffn_s,
            k_p, v_p, logf_p,
            k_s.reshape(ns, 1, N_HEADS, HEAD_DIM), v_s.reshape(ns, 1, N_HEADS, HEAD_DIM),
            lf_s[:, None, :])
```

```python
import functools

import jax
import jax.numpy as jnp
from jax import lax
from jax.experimental import pallas as pl
from jax.experimental.pallas import tpu as pltpu

f32 = jnp.float32
bf16 = jnp.bfloat16

D_MODEL = 1024
N_HEADS = 16
HEAD_DIM = D_MODEL // N_HEADS
D_FF = 2816
PAGE = 128
EPS = 1e-6
ATTN_SCALE = HEAD_DIM ** -0.5
LOG2E = 1.4426950408889634

TM = 256
TMX = 512
TMF = 256
CHUNK = 256
AUG = 16
HA = HEAD_DIM + AUG
ATT_HG = 2
ADA_TN = 1024
VMEM_LIMIT = 56 * 1024 * 1024
DEC_ROWS = (4, 8, 4, 8, 8)


def _cparams(sem, vmem=None):
    return pltpu.CompilerParams(dimension_semantics=sem, vmem_limit_bytes=vmem)


def _resident(shape):
    nd = len(shape)
    return pl.BlockSpec(shape, lambda *_: (0,) * nd, pipeline_mode=pl.Buffered(1))


def _rms_rows(x):
    return x * lax.rsqrt(jnp.mean(x * x, axis=-1, keepdims=True) + EPS)


def _modulate_rows(x, g, shift, scale):
    return (_rms_rows(x) * g) * (1.0 + scale) + shift


def _gelu_tanh(x):
    c = 0.7978845608028654
    return x * (0.5 * (1.0 + jnp.tanh(c * (x + 0.044715 * (x * x * x)))))


def _log_sigmoid(x):
    return jnp.minimum(x, 0.0) - jnp.log1p(jnp.exp(-jnp.abs(x)))


def _split3(x):
    hi = x.astype(bf16)
    r1 = x - hi.astype(f32)
    mid = r1.astype(bf16)
    lo = (r1 - mid.astype(f32)).astype(bf16)
    return hi, mid, lo


def _dot3(x, m_bf16):
    hi, mid, lo = _split3(x)
    d = functools.partial(jnp.dot, preferred_element_type=f32)
    return (d(hi, m_bf16) + d(mid, m_bf16)) + d(lo, m_bf16)


def _ada_body(c_ref, w_ref, b_ref, o_ref):
    c = c_ref[...]
    s = c * (1.0 / (1.0 + jnp.exp(-c)))
    o_ref[0] = jnp.dot(s.astype(bf16), w_ref[0].astype(bf16), preferred_element_type=f32) + b_ref[0]


def _ada(c_all, w_ada, b_ada):
    n_layers, _, n_out = w_ada.shape
    rows = c_all.shape[0]
    return pl.pallas_call(
        _ada_body,
        out_shape=jax.ShapeDtypeStruct((n_layers, rows, n_out), f32),
        grid=(n_layers, n_out // ADA_TN),
        in_specs=[
            pl.BlockSpec((rows, D_MODEL), lambda l, j: (0, 0)),
            pl.BlockSpec((1, D_MODEL, ADA_TN), lambda l, j: (l, 0, j)),
            pl.BlockSpec((1, 1, ADA_TN), lambda l, j: (l, 0, j)),
        ],
        out_specs=pl.BlockSpec((1, rows, ADA_TN), lambda l, j: (l, 0, j)),
        compiler_params=_cparams(("parallel", "parallel")),
        name="ada",
    )(c_all, w_ada, b_ada.reshape(n_layers, 1, n_out))


def _own_lanes():
    shape = (N_HEADS, D_MODEL)
    return lax.broadcasted_iota(jnp.int32, shape, 1) // HEAD_DIM == lax.broadcasted_iota(jnp.int32, shape, 0)


def _block_diag(row):
    return jnp.where(_own_lanes(), jnp.broadcast_to(row, (N_HEADS, D_MODEL)), 0.0)


def _decode_init(o_ref, scratch):
    m_ref, l_ref, suf_ref, acc_ref = scratch
    m_ref[...] = jnp.full(m_ref.shape, -jnp.inf, f32)
    l_ref[...] = jnp.zeros(l_ref.shape, f32)
    suf_ref[...] = jnp.zeros(suf_ref.shape, f32)
    acc_ref[...] = jnp.zeros(acc_ref.shape, f32)
    o_ref[...] = jnp.zeros(o_ref.shape, f32)


def _decode_scores(q_ref, fn_ref, rev_ref, k_page, lf_page, scratch, g):
    m_ref, l_ref, suf_ref, _ = scratch
    fn = fn_ref[0]
    qbd_b = _block_diag(q_ref[0]).astype(bf16)
    carry = suf_ref[...]
    lf_all = jnp.concatenate([lf_page(j) for j in range(g)], axis=0)
    suf_all = _dot3(lf_all, rev_ref[...])
    tot_all = jnp.sum(lf_all, axis=1, keepdims=True)
    scores = []
    for j0 in range(0, g, 2):
        kb = jnp.concatenate([k_page(j).reshape(D_MODEL, PAGE).astype(bf16) for j in (j0, j0 + 1)],
                             axis=1)
        s2 = jnp.dot(qbd_b, kb, preferred_element_type=f32) * ATTN_SCALE
        for j in (j0, j0 + 1):
            rows = slice(j * N_HEADS, (j + 1) * N_HEADS)
            suf = suf_all[rows] + carry
            carry = carry + tot_all[rows]
            scores.append(s2[:, (j - j0) * PAGE:(j - j0 + 1) * PAGE] + fn + suf)
    suf_ref[...] = carry

    m_old = m_ref[...]
    m_new = m_old
    for s in scores:
        m_new = jnp.maximum(m_new, jnp.max(s, axis=1, keepdims=True))
    alpha = jnp.exp(m_old - m_new)
    probs = [jnp.exp(s - m_new) for s in scores]
    l_new = l_ref[...] * alpha
    for p in probs:
        l_new = l_new + jnp.sum(p, axis=1, keepdims=True)
    m_ref[...] = m_new
    l_ref[...] = l_new
    return probs, alpha


def _decode_values(v_page, scratch, g, probs, alpha):
    acc_ref = scratch[3]
    alpha_b = jnp.broadcast_to(alpha, (N_HEADS, PAGE))
    for h in range(N_HEADS):
        a = acc_ref[h] * alpha_b[h:h + 1, :]
        for j in range(g):
            a = a + v_page(j, h) * probs[j][h:h + 1, :]
        acc_ref[h] = a


def _decode_finish(ins, o_ref, scratch):
    q_ref, kn_ref, vn_ref, fn_ref = ins[:4]
    m_ref, l_ref, _, acc_ref = scratch
    fn = fn_ref[0]
    own = _own_lanes()
    shape = (N_HEADS, D_MODEL)

    def spread(c):
        return jnp.sum(jnp.where(own, jnp.broadcast_to(c, shape), 0.0), axis=0, keepdims=True)

    qbd = _block_diag(q_ref[0])
    s_new = jnp.sum(qbd * jnp.broadcast_to(kn_ref[0], shape), axis=1, keepdims=True) * ATTN_SCALE
    s_new = (s_new + fn) - fn
    m_old = m_ref[...]
    m_fin = jnp.maximum(m_old, s_new)
    a_fin = jnp.exp(m_old - m_fin)
    p_new = jnp.exp(s_new - m_fin)
    l_fin = l_ref[...] * a_fin + p_new
    o_col = jnp.sum(acc_ref[...], axis=2, keepdims=True).reshape(D_MODEL, 1)
    o_row = jnp.broadcast_to(o_col, (D_MODEL, 128)).T[0:1, :]
    o_ref[0] = (o_row * spread(a_fin) + vn_ref[0] * spread(p_new)) / spread(l_fin)


def _host_call(name, pre, main, grid, in_specs, args, out_specs, out_shapes, scratch, dec):
    pt, q, kn, vn, fn, rev, ckt, cvt, clft = dec
    rows, n_pages = pt.shape
    nj = grid[1]
    steps = grid[0] * nj
    assert steps % rows == 0 and n_pages % (steps // rows) == 0
    spr = steps // rows
    g = n_pages // spr
    n_in, n_out, n_scr, n_dec = len(args), len(out_shapes), len(scratch), 8

    def row_of(i, j):
        return (i * nj + j) // spr

    row_spec = pl.BlockSpec((1, 1, D_MODEL), lambda i, j, ptr: (row_of(i, j), 0, 0))
    hbm_spec = pl.BlockSpec(memory_space=pl.ANY)
    dec_specs = [row_spec, row_spec, row_spec,
                 pl.BlockSpec((1, N_HEADS, 1), lambda i, j, ptr: (row_of(i, j), 0, 0)),
                 pl.BlockSpec((PAGE, PAGE), lambda i, j, ptr: (0, 0)),
                 hbm_spec, hbm_spec, hbm_spec]
    dec_args = [q[:, None, :], kn[:, None, :], vn[:, None, :], fn[:, :, None], rev, ckt, cvt, clft]
    dec_scratch = [
        pltpu.VMEM((N_HEADS, 1), f32),
        pltpu.VMEM((N_HEADS, 1), f32),
        pltpu.VMEM((N_HEADS, 1), f32),
        pltpu.VMEM((N_HEADS, HEAD_DIM, PAGE), f32),
        pltpu.VMEM((2 * g, N_HEADS, HEAD_DIM, PAGE), f32),
        pltpu.VMEM((2 * g, N_HEADS, HEAD_DIM, PAGE), f32),
        pltpu.VMEM((2 * g, N_HEADS, PAGE), f32),
        pltpu.SemaphoreType.DMA((2, 3, g)),
    ]

    def body(pt_ref, *refs):
        host_in, dec_in = refs[:n_in], refs[n_in:n_in + n_dec]
        rest = refs[n_in + n_dec:]
        host_out, o_dec = rest[:n_out], rest[n_out]
        host_scr, dec_rest = rest[n_out + 1:n_out + 1 + n_scr], rest[n_out + 1 + n_scr:]
        dec_scr, (kbuf, vbuf, lfbuf, sem) = dec_rest[:4], dec_rest[4:]
        q_ref, _, _, fn_ref, rev_ref, ckt_ref, cvt_ref, clft_ref = dec_in
        host_refs = (*host_in, *host_out, *host_scr)
        lin = pl.program_id(0) * nj + pl.program_id(1)
        pstep = lin % spr
        slot = lin % 2

        def page_copies(step, to_slot):
            row = step // spr
            first_page = n_pages - 1 - (step % spr) * g
            copies = []
            for jj in range(g):
                phys = pt_ref[row, first_page - jj]
                dst = to_slot * g + jj
                copies.append(pltpu.make_async_copy(ckt_ref.at[phys], kbuf.at[dst], sem.at[to_slot, 0, jj]))
                copies.append(pltpu.make_async_copy(cvt_ref.at[phys], vbuf.at[dst], sem.at[to_slot, 1, jj]))
                copies.append(pltpu.make_async_copy(clft_ref.at[phys], lfbuf.at[dst], sem.at[to_slot, 2, jj]))
            return copies

        def start_all(copies):
            for n, c in enumerate(copies):
                c.start(priority=0 if n % 3 == 0 else 1)

        @pl.when(lin == 0)
        def _():
            start_all(page_copies(0, 0))

        @pl.when(lin + 1 < steps)
        def _():
            start_all(page_copies(lin + 1, 1 - slot))

        if pre is not None:
            pre(*host_refs)

        @pl.when(pstep == 0)
        def _():
            _decode_init(o_dec, dec_scr)

        for c in page_copies(lin, slot):
            c.wait()

        probs, alpha = _decode_scores(q_ref, fn_ref, rev_ref, lambda j: kbuf[slot * g + j],
                                      lambda j: lfbuf[slot * g + j], dec_scr, g)
        main(*host_refs)

        p_max = functools.reduce(jnp.maximum, probs)

        @pl.when(jnp.max(p_max) > 0.0)
        def _():
            _decode_values(lambda j, h: vbuf[slot * g + j, h], dec_scr, g, probs, alpha)

        @pl.when(pstep == spr - 1)
        def _():
            _decode_finish(dec_in, o_dec, dec_scr)

    grid_spec = pltpu.PrefetchScalarGridSpec(
        num_scalar_prefetch=1,
        grid=grid,
        in_specs=list(in_specs) + dec_specs,
        out_specs=(*out_specs, row_spec),
        scratch_shapes=list(scratch) + dec_scratch,
    )
    outs = pl.pallas_call(
        body,
        out_shape=(*out_shapes, jax.ShapeDtypeStruct((rows, 1, D_MODEL), f32)),
        grid_spec=grid_spec,
        compiler_params=_cparams(("arbitrary", "arbitrary"), VMEM_LIMIT),
        name=name,
    )(pt, *args, *dec_args)
    return outs[:-1], outs[-1][:, 0, :]


def _conv_start(buf):
    @pl.when(pl.program_id(1) == 0)
    def _():
        buf[0:8, :] = jnp.zeros((8, buf.shape[1]), f32)


def _conv_chunk(u, cw, buf, c0):
    tm, w = u.shape
    cols = slice(c0, c0 + w)
    buf[8:tm + 8, cols] = u
    return (buf[6:tm + 6, cols] * cw[0:1, cols] + buf[7:tm + 7, cols] * cw[1:2, cols]
            + u * cw[2:3, cols])


def _conv_finish(buf, st_ref, tm):
    st_ref[0] = buf[tm + 6:tm + 8, :]
    buf[0:8, :] = buf[tm:tm + 8, :]


def _chunk_pipeline(n, produce, consume):
    pending = [produce(0)]
    total = None
    for c in range(n):
        if c + 1 < n:
            pending.append(produce(c + 1))
        part = consume(c, *pending.pop(0))
        total = part if total is None else total + part
    return total


def _mix_prompt_pre(x_ref, mod_ref, g_ref, win_ref, cw_ref, wout_ref, o_ref, st_ref, ubuf):
    _conv_start(ubuf)


def _mix_prompt_main(x_ref, mod_ref, g_ref, win_ref, cw_ref, wout_ref, o_ref, st_ref, ubuf):
    d = D_MODEL
    tm = x_ref.shape[1]
    x = x_ref[0]
    m = mod_ref[0]
    h = _modulate_rows(x, g_ref[...], m[0:1], m[1:2]).astype(bf16)
    z = jnp.dot(h, win_ref[...], preferred_element_type=f32)
    uc = _conv_chunk(z[:, d:2 * d] * z[:, 2 * d:], cw_ref[...], ubuf, 0)
    _conv_finish(ubuf, st_ref, tm)
    y = jnp.dot((z[:, :d] * uc).astype(bf16), wout_ref[...], preferred_element_type=f32)
    o_ref[0] = x + m[2:3] * y


def _mix_prompt(x, mod, g, w_in, cw, w_out, dec):
    b, t, d = x.shape
    xspec = pl.BlockSpec((1, TMX, d), lambda i, j, *_: (i, j, 0))
    return _host_call(
        "mix_prompt", _mix_prompt_pre, _mix_prompt_main, (b, t // TMX),
        in_specs=[
            xspec,
            pl.BlockSpec((1, 6, d), lambda i, j, *_: (i, 0, 0)),
            pl.BlockSpec((1, d), lambda i, j, *_: (0, 0)),
            _resident((d, 3 * d)),
            pl.BlockSpec((3, d), lambda i, j, *_: (0, 0)),
            _resident((d, d)),
        ],
        args=[x, mod, g, w_in, cw, w_out],
        out_specs=(xspec, pl.BlockSpec((1, 2, d), lambda i, j, *_: (i, 0, 0))),
        out_shapes=(jax.ShapeDtypeStruct((b, t, d), f32), jax.ShapeDtypeStruct((b, 2, d), f32)),
        scratch=[pltpu.VMEM((TMX + 8, d), f32)],
        dec=dec)


def _mix_sample_body(x_ref, mod_ref, g_ref, p0_ref, p1_ref, win_ref, cw_ref, wout_ref, o_ref, u_ref):
    d = D_MODEL
    x = x_ref[...]
    h = _modulate_rows(x, g_ref[...], mod_ref[0], mod_ref[1])
    z = jnp.dot(h.astype(bf16), win_ref[...], preferred_element_type=f32)
    u = z[:, d:2 * d] * z[:, 2 * d:]
    cw = cw_ref[...]
    uc = p0_ref[...] * cw[0:1] + p1_ref[...] * cw[1:2] + u * cw[2:3]
    u_ref[...] = u
    y = jnp.dot((z[:, :d] * uc).astype(bf16), wout_ref[...], preferred_element_type=f32)
    o_ref[...] = x + mod_ref[2] * y


def _mix_sample(x, mod, g, past0, past1, w_in, cw, w_out):
    n, d = x.shape
    return pl.pallas_call(
        _mix_sample_body,
        out_shape=(jax.ShapeDtypeStruct((n, d), f32), jax.ShapeDtypeStruct((n, d), f32)),
        compiler_params=_cparams(None, VMEM_LIMIT),
        name="mix_sample",
    )(x, mod, g, past0, past1, w_in, cw, w_out)


def _ffn_prompt_pre(with_attn, *refs):
    _conv_start(refs[-1])


def _ffn_prompt_main(with_attn, *refs):
    if with_attn:
        (x_ref, ot_ref, wo_ref, mod_ref, g_ref, wup_ref, cw_ref, wdown_ref, o_ref, st_ref, abuf) = refs
    else:
        (x_ref, mod_ref, g_ref, wup_ref, cw_ref, wdown_ref, o_ref, st_ref, abuf) = refs
    tm = x_ref.shape[1]
    x = x_ref[0]
    m = mod_ref[0]
    if with_attn:
        attn = lax.dot_general(ot_ref[0], wo_ref[...], (((0,), (0,)), ((), ())),
                               preferred_element_type=f32)
        x = x + m[2:3] * attn
    h = _modulate_rows(x, g_ref[...], m[3:4], m[4:5]).astype(bf16)
    cw = cw_ref[...]
    dot = functools.partial(jnp.dot, preferred_element_type=f32)

    def up(c):
        c0 = c * CHUNK
        return dot(h, wup_ref[:, c0:c0 + CHUNK]), dot(h, wup_ref[:, D_FF + c0:D_FF + c0 + CHUNK])

    def down(c, a, v):
        c0 = c * CHUNK
        gl = _gelu_tanh(_conv_chunk(a, cw, abuf, c0)) * v
        return dot(gl.astype(bf16), wdown_ref[c0:c0 + CHUNK, :])

    y = _chunk_pipeline(D_FF // CHUNK, up, down)
    _conv_finish(abuf, st_ref, tm)
    o_ref[0] = x + m[5:6] * y


def _ffn_prompt(x, mod, g, w_up, cw, w_down, dec, o_t=None, w_o=None):
    b, t, d = x.shape
    with_attn = o_t is not None
    xspec = pl.BlockSpec((1, TMF, d), lambda i, j, *_: (i, j, 0))
    in_specs = [xspec]
    args = [x]
    if with_attn:
        in_specs += [pl.BlockSpec((1, d, TMF), lambda i, j, *_: (i, 0, j)), _resident((d, d))]
        args += [o_t, w_o]
    in_specs += [
        pl.BlockSpec((1, 6, d), lambda i, j, *_: (i, 0, 0)),
        pl.BlockSpec((1, d), lambda i, j, *_: (0, 0)),
        _resident((d, 2 * D_FF)),
        pl.BlockSpec((3, D_FF), lambda i, j, *_: (0, 0)),
        _resident((D_FF, d)),
    ]
    args += [mod, g, w_up, cw, w_down]
    return _host_call(
        "ffn_attn_prompt" if with_attn else "ffn_prompt",
        functools.partial(_ffn_prompt_pre, with_attn), functools.partial(_ffn_prompt_main, with_attn),
        (b, t // TMF), in_specs=in_specs, args=args,
        out_specs=(xspec, pl.BlockSpec((1, 2, D_FF), lambda i, j, *_: (i, 0, 0))),
        out_shapes=(jax.ShapeDtypeStruct((b, t, d), f32), jax.ShapeDtypeStruct((b, 2, D_FF), f32)),
        scratch=[pltpu.VMEM((TMF + 8, D_FF), f32)],
        dec=dec)


def _ffn_sample_body(with_attn, *refs):
    if with_attn:
        (x_ref, o_ref_in, wo_ref, mod_ref, g_ref, p0_ref, p1_ref, wup_ref, cw_ref, wdown_ref,
         y_ref, a_ref) = refs
    else:
        (x_ref, mod_ref, g_ref, p0_ref, p1_ref, wup_ref, cw_ref, wdown_ref, y_ref, a_ref) = refs
    x = x_ref[...]
    if with_attn:
        attn = jnp.dot(o_ref_in[...].astype(bf16), wo_ref[...], preferred_element_type=f32)
        x = x + mod_ref[2] * attn
    cw = cw_ref[...]
    h = _modulate_rows(x, g_ref[...], mod_ref[3], mod_ref[4])
    up = jnp.dot(h.astype(bf16), wup_ref[...], preferred_element_type=f32)
    a = up[:, :D_FF]
    a_ref[...] = a
    ac = p0_ref[...] * cw[0:1] + p1_ref[...] * cw[1:2] + a * cw[2:3]
    gl = _gelu_tanh(ac) * up[:, D_FF:]
    y = jnp.dot(gl.astype(bf16), wdown_ref[...], preferred_element_type=f32)
    y_ref[...] = x + mod_ref[5] * y


def _ffn_sample(x, mod, g, past0, past1, w_up, cw, w_down, o=None, w_o=None):
    n, d = x.shape
    with_attn = o is not None
    args = [x] + ([o, w_o] if with_attn else []) + [mod, g, past0, past1, w_up, cw, w_down]
    return pl.pallas_call(
        functools.partial(_ffn_sample_body, with_attn),
        out_shape=(jax.ShapeDtypeStruct((n, d), f32), jax.ShapeDtypeStruct((n, D_FF), f32)),
        compiler_params=_cparams(None, VMEM_LIMIT),
        name="ffn_attn_sample" if with_attn else "ffn_sample",
    )(*args)


def _aug_rows(f_row, negate_first):
    hi, mid, lo = (p.astype(f32) for p in _split3(f_row))
    tm = f_row.shape[1]
    row = lax.broadcasted_iota(jnp.int32, (AUG, tm), 0)
    one = jnp.ones((AUG, tm), f32)
    zero = jnp.zeros((AUG, tm), f32)
    if negate_first:
        parts = (-hi, -mid, -lo, one, one, one)
    else:
        parts = (one, one, one, hi, mid, lo)
    out = zero
    for r, p in enumerate(parts):
        out = jnp.where(row == r, jnp.broadcast_to(p, (AUG, tm)), out)
    return out


def _kvq_prompt_pre(*refs):
    carry = refs[-1]

    @pl.when(pl.program_id(1) == 0)
    def _():
        carry[...] = jnp.zeros(carry.shape, f32)


def _kvq_prompt_main(x_ref, modc_ref, pcol_ref, bf_ref, tri_ref, wk_ref, wv_ref, wf_ref, wq_ref,
                     kt_ref, vt_ref, lf_ref, qa_ref, ka_ref, vb_ref, carry):
    tm = x_ref.shape[1]
    xt = x_ref[0].T
    xn = xt * lax.rsqrt(jnp.mean(xt * xt, axis=0, keepdims=True) + EPS)
    pc = pcol_ref[...]
    mc = modc_ref[0]
    hk = (xn * pc[:, 0:1]).astype(bf16)
    hq = ((xn * pc[:, 1:2]) * (1.0 + mc[:, 1:2]) + mc[:, 0:1]).astype(bf16)
    kt = jnp.dot(wk_ref[...], hk, preferred_element_type=f32)
    vt = jnp.dot(wv_ref[...], hk, preferred_element_type=f32)
    ft = jnp.dot(wf_ref[...], hk, preferred_element_type=f32)
    qt = jnp.dot(wq_ref[...], hq, preferred_element_type=f32)

    logf = _log_sigmoid(ft + bf_ref[...])
    lf_ref[0] = logf
    cum = _dot3(logf, tri_ref[...]) + carry[...]
    carry[...] = cum[:, tm - 1:tm]

    vt_ref[0] = vt
    vb_ref[0, :, 0:HEAD_DIM, :] = vt.astype(bf16).reshape(N_HEADS, HEAD_DIM, tm)
    ones_row = lax.broadcasted_iota(jnp.int32, (N_HEADS, AUG, tm), 1) == 0
    vb_ref[0, :, HEAD_DIM:HA, :] = ones_row.astype(bf16)
    for h in range(N_HEADS):
        r0, r1 = h * HEAD_DIM, (h + 1) * HEAD_DIM
        kh = kt[r0:r1]
        kh = kh * lax.rsqrt(jnp.mean(kh * kh, axis=0, keepdims=True) + EPS) * pc[r0:r1, 2:3]
        kt_ref[0, r0:r1, :] = kh
        qh = qt[r0:r1]
        qh = qh * lax.rsqrt(jnp.mean(qh * qh, axis=0, keepdims=True) + EPS) * pc[r0:r1, 3:4]
        f_row = cum[h:h + 1] * LOG2E
        ka_ref[0, h, 0:HEAD_DIM, :] = kh.astype(bf16)
        ka_ref[0, h, HEAD_DIM:HA, :] = _aug_rows(f_row, True).astype(bf16)
        qa_ref[0, h, 0:HEAD_DIM, :] = (qh * (ATTN_SCALE * LOG2E)).astype(bf16)
        qa_ref[0, h, HEAD_DIM:HA, :] = _aug_rows(f_row, False).astype(bf16)


def _kvq_prompt(x, modc, pcol, bf_col, tri, wk_t, wv_t, wf_t, wq_t, dec):
    b, t, d = x.shape
    out_shapes = (
        jax.ShapeDtypeStruct((b, d, t), f32),
        jax.ShapeDtypeStruct((b, d, t), f32),
        jax.ShapeDtypeStruct((b, N_HEADS, t), f32),
        jax.ShapeDtypeStruct((b, N_HEADS, HA, t), bf16),
        jax.ShapeDtypeStruct((b, N_HEADS, HA, t), bf16),
        jax.ShapeDtypeStruct((b, N_HEADS, HA, t), bf16),
    )
    ct = pl.BlockSpec((1, d, TM), lambda i, j, *_: (i, 0, j))
    ha = pl.BlockSpec((1, N_HEADS, HA, TM), lambda i, j, *_: (i, 0, 0, j))
    return _host_call(
        "kvq_prompt", _kvq_prompt_pre, _kvq_prompt_main, (b, t // TM),
        in_specs=[
            pl.BlockSpec((1, TM, d), lambda i, j, *_: (i, j, 0)),
            pl.BlockSpec((1, d, 2), lambda i, j, *_: (i, 0, 0)),
            pl.BlockSpec((d, 4), lambda i, j, *_: (0, 0)),
            pl.BlockSpec((N_HEADS, 1), lambda i, j, *_: (0, 0)),
            pl.BlockSpec((TM, TM), lambda i, j, *_: (0, 0)),
            _resident((d, d)), _resident((d, d)), _resident((N_HEADS, d)), _resident((d, d)),
        ],
        args=[x, modc, pcol, bf_col, tri, wk_t, wv_t, wf_t, wq_t],
        out_specs=(ct, ct, pl.BlockSpec((1, N_HEADS, TM), lambda i, j, *_: (i, 0, j)), ha, ha, ha),
        out_shapes=out_shapes,
        scratch=[pltpu.VMEM((N_HEADS, 1), f32)],
        dec=dec)


def _attn_prompt_main(qa_ref, ka_ref, vb_ref, o_ref):
    hg, _, t = qa_ref.shape[1:]
    tq = TM
    key = lax.broadcasted_iota(jnp.int32, (tq, tq), 0)
    qry = lax.broadcasted_iota(jnp.int32, (tq, tq), 1)
    causal = key <= qry
    contract0 = (((0,), (0,)), ((), ()))

    def scores(h, i):
        q0 = i * tq
        qa = qa_ref[0, h, :, q0:q0 + tq]
        s_d = lax.dot_general(ka_ref[0, h, :, q0:q0 + tq], qa, contract0,
                              preferred_element_type=f32)
        s_d = jnp.where(causal, s_d, -jnp.inf)
        m = jnp.max(s_d, axis=0, keepdims=True)
        s_o = None
        if i > 0:
            s_o = lax.dot_general(ka_ref[0, h, :, 0:q0], qa, contract0,
                                  preferred_element_type=f32)
            m = jnp.maximum(m, jnp.max(s_o, axis=0, keepdims=True))
        return s_d, s_o, m

    def weighted_values(h, i, s_d, s_o, m):
        q0 = i * tq
        p_d = jnp.exp2(s_d - m)
        acc = jnp.dot(vb_ref[0, h, :, q0:q0 + tq], p_d.astype(bf16), preferred_element_type=f32)
        if i > 0:
            p_o = jnp.exp2(s_o - m)
            acc = acc + jnp.dot(vb_ref[0, h, :, 0:q0], p_o.astype(bf16), preferred_element_type=f32)
        o_ref[0, h, :, q0:q0 + tq] = (acc[0:HEAD_DIM] / acc[HEAD_DIM:HEAD_DIM + 1]).astype(bf16)

    units = [(h, i) for h in range(hg) for i in range(t // tq)]
    ahead = 3
    pending = [scores(*u) for u in units[:ahead]]
    for n, unit in enumerate(units):
        if n + ahead < len(units):
            pending.append(scores(*units[n + ahead]))
        weighted_values(*unit, *pending.pop(0))


def _attn_prompt(qa, ka, vb, dec):
    b, nh, _, t = qa.shape
    spec_a = pl.BlockSpec((1, ATT_HG, HA, t), lambda i, j, *_: (i, j, 0, 0))
    spec_o = pl.BlockSpec((1, ATT_HG, HEAD_DIM, t), lambda i, j, *_: (i, j, 0, 0))
    return _host_call(
        "attn_prompt", None, _attn_prompt_main, (b, nh // ATT_HG),
        in_specs=[spec_a, spec_a, spec_a], args=[qa, ka, vb],
        out_specs=(spec_o,), out_shapes=(jax.ShapeDtypeStruct((b, nh, HEAD_DIM, t), bf16),),
        scratch=[], dec=dec)


def _head_rms_rows(x, gmat_ref, w_row):
    ms = _dot3(x * x, gmat_ref[...])
    return x * lax.rsqrt(ms + EPS) * w_row


def _kvq_sample_body(x_ref, mod_ref, prow_ref, bf_ref, gmat_ref, wkvf_ref, wq_ref,
                     k_ref, v_ref, lf_ref, q_ref):
    d = D_MODEL
    xn = _rms_rows(x_ref[...])
    pr = prow_ref[...]
    hk = (xn * pr[0:1]).astype(bf16)
    hq = ((xn * pr[1:2]) * (1.0 + mod_ref[1]) + mod_ref[0]).astype(bf16)
    kvf = jnp.dot(hk, wkvf_ref[...], preferred_element_type=f32)
    qr = jnp.dot(hq, wq_ref[...], preferred_element_type=f32)
    k_ref[...] = _head_rms_rows(kvf[:, :d], gmat_ref, pr[2:3])
    v_ref[...] = kvf[:, d:2 * d]
    lf_ref[...] = _log_sigmoid(kvf[:, 2 * d:] + bf_ref[...])
    q_ref[...] = _head_rms_rows(qr, gmat_ref, pr[3:4])


def _kvq_sample(x, mod, prow, bf_row, gmat, w_kvf_pad, w_q):
    n, d = x.shape
    return pl.pallas_call(
        _kvq_sample_body,
        out_shape=(jax.ShapeDtypeStruct((n, d), f32), jax.ShapeDtypeStruct((n, d), f32),
                   jax.ShapeDtypeStruct((n, 128), f32), jax.ShapeDtypeStruct((n, d), f32)),
        compiler_params=_cparams(None, VMEM_LIMIT),
        name="kvq_sample",
    )(x, mod, prow, bf_row, gmat, w_kvf_pad, w_q)


def kernel(x_prompt, x_sample, state_conv_a, state_ffn, cache_k, cache_v, cache_logf, page_table,
           c_prompt, c_sample, norm_mix, norm_ffn, w_ada, b_ada, w_in_a, conv_w_a, w_out_a, kv_norm,
           w_kvf, b_f, k_norm, w_q, q_norm, w_o, w_up, conv_w_ffn, w_down):
    d = D_MODEL
    nb, t, _ = x_prompt.shape
    ns = x_sample.shape[0]
    assert sum(DEC_ROWS) == ns

    w_in = w_in_a[0].astype(bf16)
    w_out = w_out_a[0].astype(bf16)
    w_up_b = w_up.astype(bf16)
    w_down_b = w_down.astype(bf16)
    w_o_b = w_o[0].astype(bf16)
    w_q_b = w_q[0].astype(bf16)
    w_kvf_b = w_kvf.astype(bf16)
    wk_t = w_kvf_b[:, :d].T
    wv_t = w_kvf_b[:, d:2 * d].T
    wf_t = w_kvf_b[:, 2 * d:].T
    wq_t = w_q_b.T
    w_kvf_pad = jnp.pad(w_kvf_b, ((0, 0), (0, 128 - N_HEADS)))
    k_norm_d = jnp.tile(k_norm, N_HEADS)
    q_norm_d = jnp.tile(q_norm[0], N_HEADS)
    prow = jnp.stack([kv_norm, norm_mix[1], k_norm_d, q_norm_d])
    pcol = prow.T
    bf_row = jnp.pad(b_f, (0, 128 - N_HEADS))[None, :]
    bf_col = b_f[:, None]
    idx = jnp.arange(TM)
    tri = (idx[:, None] <= idx[None, :]).astype(bf16)
    pidx = jnp.arange(PAGE)
    rev = (pidx[:, None] > pidx[None, :]).astype(bf16)
    lane_head = jnp.arange(d) // HEAD_DIM
    gmat = ((lane_head[:, None] == lane_head[None, :]).astype(f32) / HEAD_DIM).astype(bf16)

    mod = _ada(jnp.concatenate([c_prompt, c_sample], axis=0), w_ada, b_ada)
    mod_p = mod[:, :nb].reshape(2, nb, 6, d)
    mod_s = mod[:, nb:].reshape(2, ns, 6, d).transpose(0, 2, 1, 3)
    modc_p = mod_p[1, :, 0:2].transpose(0, 2, 1)

    xs = x_sample[:, 0, :]
    xs1, u_s = _mix_sample(xs, mod_s[0], norm_mix[0:1], state_conv_a[0, :, 0], state_conv_a[0, :, 1],
                           w_in, conv_w_a[0], w_out)
    xs2, a0_s = _ffn_sample(xs1, mod_s[0], norm_ffn[0:1], state_ffn[0, :, 0], state_ffn[0, :, 1],
                            w_up_b[0], conv_w_ffn[0], w_down_b[0])
    k_s, v_s, lf_s, q_s = _kvq_sample(xs2, mod_s[1], prow, bf_row, gmat, w_kvf_pad, w_q_b)
    lf_s = lf_s[:, :N_HEADS]
    cache_kt = cache_k.transpose(0, 2, 3, 1)
    cache_vt = cache_v.transpose(0, 2, 3, 1)
    cache_lft = cache_logf.transpose(0, 2, 1)

    def dec_job(k):
        r0 = sum(DEC_ROWS[:k])
        rows = slice(r0, r0 + DEC_ROWS[k])
        return (page_table[rows], q_s[rows], k_s[rows], v_s[rows], lf_s[rows], rev,
                cache_kt, cache_vt, cache_lft)

    (x1, conv_a_p), o_s0 = _mix_prompt(x_prompt, mod_p[0], norm_mix[0:1], w_in, conv_w_a[0], w_out,
                                       dec_job(0))
    (x2, ffn0_p), o_s1 = _ffn_prompt(x1, mod_p[0], norm_ffn[0:1], w_up_b[0], conv_w_ffn[0], w_down_b[0],
                                     dec_job(1))
    (kt_p, vt_p, lft_p, qa, ka, vb), o_s2 = _kvq_prompt(x2, modc_p, pcol, bf_col, tri, wk_t, wv_t, wf_t,
                                                        wq_t, dec_job(2))
    (o_t,), o_s3 = _attn_prompt(qa, ka, vb, dec_job(3))
    (y_p, ffn1_p), o_s4 = _ffn_prompt(x2, mod_p[1], norm_ffn[1:2], w_up_b[1], conv_w_ffn[1], w_down_b[1],
                                      dec_job(4), o_t=o_t.reshape(nb, d, t), w_o=w_o_b)

    o_s = jnp.concatenate([o_s0, o_s1, o_s2, o_s3, o_s4], axis=0)
    y_s, a1_s = _ffn_sample(xs2, mod_s[1], norm_ffn[1:2], state_ffn[1, :, 0], state_ffn[1, :, 1],
                            w_up_b[1], conv_w_ffn[1], w_down_b[1], o=o_s, w_o=w_o_b)

    conv_a_s = jnp.stack([state_conv_a[0, :, 1], u_s], axis=1)[None]
    ffn_p = jnp.stack([ffn0_p, ffn1_p])
    ffn_s = jnp.stack([jnp.stack([state_ffn[0, :, 1], a0_s], axis=1),
                       jnp.stack([state_ffn[1, :, 1], a1_s], axis=1)])
    k_p = kt_p.reshape(nb, N_HEADS, HEAD_DIM, t).transpose(0, 3, 1, 2)
    v_p = vt_p.reshape(nb, N_HEADS, HEAD_DIM, t).transpose(0, 3, 1, 2)
    logf_p = lft_p.transpose(0, 2, 1)
    return (y_p, y_s[:, None, :], conv_a_p[None], conv_a_s, ffn_p, ffn_s,
            k_p, v_p, logf_p,
            k_s.reshape(ns, 1, N_HEADS, HEAD_DIM), v_s.reshape(ns, 1, N_HEADS, HEAD_DIM),
            lf_s[:, None, :])
```
